```python
import jax, jax.numpy as jnp
from jax import lax
import numpy as np

D_MODEL = 1024
BATCH = 2
SEQ = 8192
DEPTH = 4
DEC_BATCH = 8
DEC_SEQ = 16
PAST_LEN = 2048

CHUNK = 64
N_META = 16
Q_BLOCK = 128
ROPE_THETA = 10000.0
EPS = 1e-6
NEG = -1e30
PAD_CHUNK = 2 ** 30
A_HEADS = 4
A_DH = 64
A_VDIM = 2 * A_DH
A_WIDTH = A_HEADS * A_VDIM
B_HEADS = 8
B_DH = 64
B_WIDTH = B_HEADS * B_DH
AB_IN = 4 * A_WIDTH + 4 * B_WIDTH
C_HEADS = 16
C_DH = 64
C_WIDTH = C_HEADS * C_DH
IDX_HEADS = 4
IDX_DH = 64
TOPK_MAX = 256
C_SPLITS = (C_WIDTH, 2 * C_WIDTH, 3 * C_WIDTH, 4 * C_WIDTH,
            4 * C_WIDTH + IDX_HEADS * IDX_DH, 4 * C_WIDTH + IDX_HEADS * IDX_DH + IDX_DH)
C_IN = C_SPLITS[-1] + IDX_HEADS
N_AB = (DEPTH + 1) // 2
N_C = DEPTH // 2

kernel_name = 'hybrid_diff_stickbreak_dsa_stream_step'


def rms_norm(x, g):
    xf = x.astype(jnp.float32)
    y = xf * lax.rsqrt(jnp.mean(xf * xf, axis=-1, keepdims=True) + EPS)
    return (y * g.astype(jnp.float32)).astype(x.dtype)


def rope(x, pos):
    half = x.shape[-1] // 2
    inv_freq = ROPE_THETA ** (-jnp.arange(half, dtype=jnp.float32) / half)
    ang = pos.astype(jnp.float32)[:, None] * inv_freq[None, :]
    shape = (ang.shape[0],) + (1,) * (x.ndim - 3) + (half,)
    cos = jnp.cos(ang).reshape(shape)
    sin = jnp.sin(ang).reshape(shape)
    xf = x.astype(jnp.float32)
    x1, x2 = xf[..., :half], xf[..., half:]
    return jnp.concatenate([x1 * cos - x2 * sin, x2 * cos + x1 * sin], axis=-1).astype(x.dtype)


def sweep(fn, batched, vectors):
    L = vectors[0].shape[0]
    if L <= Q_BLOCK:
        return fn(*batched, *vectors)
    nb = L // Q_BLOCK
    bs = tuple(a.reshape(a.shape[0], nb, Q_BLOCK, *a.shape[2:]).swapaxes(0, 1) for a in batched)
    vs = tuple(v.reshape(nb, Q_BLOCK) for v in vectors)
    out = lax.map(lambda t: fn(*t[0], *t[1]), (bs, vs))
    return out.swapaxes(0, 1).reshape(out.shape[1], L, *out.shape[3:])


def diff_attention(q, qcid, k, v, kcid, lam):
    qf = q.reshape(*q.shape[:3], 2, A_DH)
    kf = k.reshape(*k.shape[:3], 2, A_DH)
    s = jnp.einsum('bqhmd,bkhmd->bhmqk', qf, kf).astype(jnp.float32) * (A_DH ** -0.5)
    vis = kcid[None, :] <= qcid[:, None]
    s = jnp.where(vis, s, NEG)
    p = jax.nn.softmax(s, axis=-1)
    w = p[:, :, 0] - lam * p[:, :, 1]
    return jnp.einsum('bhqk,bkhd->bqhd', w.astype(v.dtype), v)


def stick_breaking(q, qpos, k, v, kpos):
    z = jnp.einsum('bqhd,bkhd->bhqk', q, k).astype(jnp.float32) * (B_DH ** -0.5)
    vis = kpos[None, :] < qpos[:, None]
    log_beta = jax.nn.log_sigmoid(z)
    log_keep = jnp.where(vis, jax.nn.log_sigmoid(-z), 0.0)
    after = lax.cumsum(log_keep, axis=3, reverse=True) - log_keep
    a = jnp.where(vis, jnp.exp(log_beta + after), 0.0)
    return jnp.einsum('bhqk,bkhd->bqhd', a.astype(v.dtype), v)


def dsa_attention(q, qi, wi, qcid, k, v, ki, kcid, n_sel):
    rel = jax.nn.relu(jnp.einsum('bqhd,bkd->bqhk', qi, ki).astype(jnp.float32) * (IDX_DH ** -0.5))
    score = jnp.einsum('bqh,bqhk->bqk', wi.astype(jnp.float32) * (IDX_HEADS ** -0.5), rel)
    vis = kcid[None, :] <= qcid[:, None]
    score = jnp.where(vis[None], score, NEG)
    _, idx = lax.top_k(score, n_sel)
    sel_vis = kcid[idx] <= qcid[None, :, None]
    k_sel = jax.vmap(lambda a, i: a[i])(k, idx)
    v_sel = jax.vmap(lambda a, i: a[i])(v, idx)
    s = jnp.einsum('bqhd,bqnhd->bhqn', q, k_sel).astype(jnp.float32) * (C_DH ** -0.5)
    s = jnp.where(sel_vis[:, None], s, NEG)
    p = jax.nn.softmax(s, axis=-1)
    return jnp.einsum('bhqn,bqnhd->bqhd', p.astype(v.dtype), v_sel)


def ab_layer(x, qpos, qcid, kpos, kcid, past, g_norm, w_in, g_qk, lam_p, g_sub, w_out, lam_init):
    B, L, _ = x.shape
    h = rms_norm(x, g_norm)
    qa, ka, va, ga, qb, kb, vb, gb = jnp.split(h @ w_in, 8, axis=-1)
    qa = rope(rms_norm(qa.reshape(B, L, A_HEADS, 2, A_DH), g_qk[0]), qpos).reshape(B, L, A_HEADS, 2 * A_DH)
    ka = rope(rms_norm(ka.reshape(B, L, A_HEADS, 2, A_DH), g_qk[1]), qpos).reshape(B, L, A_HEADS, 2 * A_DH)
    va = va.reshape(B, L, A_HEADS, A_VDIM)
    qb = qb.reshape(B, L, B_HEADS, B_DH)
    kb = kb.reshape(B, L, B_HEADS, B_DH)
    vb = vb.reshape(B, L, B_HEADS, B_DH)
    new_rows = (ka, va, kb, vb)
    if past is None:
        ka_all, va_all, kb_all, vb_all = new_rows
    else:
        ka_all, va_all, kb_all, vb_all = (jnp.concatenate([c, n], axis=1) for c, n in zip(past, new_rows))
    lp = lam_p.astype(jnp.float32)
    lam = jnp.exp(jnp.sum(lp[0] * lp[1])) - jnp.exp(jnp.sum(lp[2] * lp[3])) + lam_init
    o_a = sweep(lambda qq, qc: diff_attention(qq, qc, ka_all, va_all, kcid, lam), (qa,), (qcid,))
    o_b = sweep(lambda qq, qp: stick_breaking(qq, qp, kb_all, vb_all, kpos), (qb,), (qpos,))
    o_a = rms_norm(o_a, g_sub) * (1.0 - lam_init)
    mixed = jnp.concatenate([o_a.reshape(B, L, A_WIDTH) * jax.nn.silu(ga),
                             o_b.reshape(B, L, B_WIDTH) * jax.nn.silu(gb)], axis=-1)
    return x + mixed @ w_out, new_rows


def c_layer(x, qpos, qcid, kcid, past, n_sel, g_norm, w_in, g_qk, w_out):
    B, L, _ = x.shape
    h = rms_norm(x, g_norm)
    q, k, v, g, qi, ki, wi = jnp.split(h @ w_in, C_SPLITS, axis=-1)
    q = rope(rms_norm(q.reshape(B, L, C_HEADS, C_DH), g_qk[0]), qpos)
    k = rope(rms_norm(k.reshape(B, L, C_HEADS, C_DH), g_qk[1]), qpos)
    v = v.reshape(B, L, C_HEADS, C_DH)
    qi = rope(qi.reshape(B, L, IDX_HEADS, IDX_DH), qpos)
    ki = rope(ki, qpos)
    new_rows = (k, v, ki)
    if past is None:
        k_all, v_all, ki_all = new_rows
    else:
        k_all, v_all, ki_all = (jnp.concatenate([c, n], axis=1) for c, n in zip(past, new_rows))
    o = sweep(lambda qq, qqi, wwi, qc: dsa_attention(qq, qqi, wwi, qc, k_all, v_all, ki_all, kcid, n_sel),
              (q, qi, wi), (qcid,))
    mixed = o.reshape(B, L, C_WIDTH) * jax.nn.silu(g)
    return x + mixed @ w_out, new_rows


def setup_inputs(seed: int = 0) -> dict:
    key = jax.random.key(seed)
    ks = jax.random.split(key, 24)
    f32 = jnp.float32

    def nrm(k, shape, s=1.0):
        return jax.random.normal(k, shape, f32) * s

    def gain(k, shape):
        return 1.0 + 0.02 * jax.random.normal(k, shape, f32)

    return {
        'x_prompt': nrm(ks[0], (BATCH, SEQ, D_MODEL)),
        'x_sample': nrm(ks[1], (DEC_BATCH, DEC_SEQ, D_MODEL)),
        'cache_a_k': nrm(ks[2], (N_AB, DEC_BATCH, PAST_LEN, A_HEADS, 2 * A_DH)),
        'cache_a_v': nrm(ks[3], (N_AB, DEC_BATCH, PAST_LEN, A_HEADS, A_VDIM)),
        'cache_b_k': nrm(ks[4], (N_AB, DEC_BATCH, PAST_LEN, B_HEADS, B_DH)),
        'cache_b_v': nrm(ks[5], (N_AB, DEC_BATCH, PAST_LEN, B_HEADS, B_DH)),
        'cache_c_k': nrm(ks[6], (N_C, DEC_BATCH, PAST_LEN, C_HEADS, C_DH)),
        'cache_c_v': nrm(ks[7], (N_C, DEC_BATCH, PAST_LEN, C_HEADS, C_DH)),
        'cache_c_kidx': nrm(ks[8], (N_C, DEC_BATCH, PAST_LEN, IDX_DH)),
        'meta_tokens': nrm(ks[9], (N_META, D_MODEL)),
        'g_norm_ab': gain(ks[10], (N_AB, D_MODEL)),
        'w_in_ab': nrm(ks[11], (N_AB, D_MODEL, AB_IN), D_MODEL ** -0.5),
        'g_qk_a': gain(ks[12], (N_AB, 2, A_DH)),
        'lam_a': nrm(ks[13], (N_AB, 4, A_DH), 0.1),
        'g_sub_a': gain(ks[14], (N_AB, A_VDIM)),
        'w_out_ab': nrm(ks[15], (N_AB, A_WIDTH + B_WIDTH, D_MODEL), (A_WIDTH + B_WIDTH) ** -0.5),
        'g_norm_c': gain(ks[16], (N_C, D_MODEL)),
        'w_in_c': nrm(ks[17], (N_C, D_MODEL, C_IN), D_MODEL ** -0.5),
        'g_qk_c': gain(ks[18], (N_C, 2, C_DH)),
        'w_out_c': nrm(ks[19], (N_C, C_WIDTH, D_MODEL), C_WIDTH ** -0.5),
    }


def reference(x_prompt, x_sample, cache_a_k, cache_a_v, cache_b_k, cache_b_v, cache_c_k, cache_c_v,
              cache_c_kidx, meta_tokens, g_norm_ab, w_in_ab, g_qk_a, lam_a, g_sub_a, w_out_ab,
              g_norm_c, w_in_c, g_qk_c, w_out_c):
    t_real = N_META + SEQ
    l_pad = -(-t_real // Q_BLOCK) * Q_BLOCK
    pos_p = jnp.arange(l_pad, dtype=jnp.int32)
    cid_p = jnp.where(pos_p < N_META, 0, (pos_p - N_META) // CHUNK + 1)
    cid_p = jnp.where(pos_p >= t_real, PAD_CHUNK, cid_p).astype(jnp.int32)
    meta = jnp.broadcast_to(meta_tokens.astype(x_prompt.dtype)[None], (BATCH, N_META, D_MODEL))
    hp = jnp.concatenate([meta, x_prompt, jnp.zeros((BATCH, l_pad - t_real, D_MODEL), x_prompt.dtype)], axis=1)
    n_sel_p = min(TOPK_MAX, SEQ // 4)
    pos_s = PAST_LEN + jnp.arange(DEC_SEQ, dtype=jnp.int32)
    cid_s = pos_s // CHUNK
    kpos_s = jnp.arange(PAST_LEN + DEC_SEQ, dtype=jnp.int32)
    kcid_s = kpos_s // CHUNK
    hs = x_sample
    n_sel_s = min(TOPK_MAX, (PAST_LEN + DEC_SEQ) // 4)

    rows_ab_p, rows_ab_s, rows_c_p, rows_c_s = [], [], [], []
    for l in range(DEPTH):
        if l % 2 == 0:
            i = l // 2
            lam_init = 0.8 - 0.6 * float(np.exp(-0.3 * l))
            w = (g_norm_ab[i], w_in_ab[i], g_qk_a[i], lam_a[i], g_sub_a[i], w_out_ab[i], lam_init)
            hp, rp = ab_layer(hp, pos_p, cid_p, pos_p, cid_p, None, *w)
            past = (cache_a_k[i], cache_a_v[i], cache_b_k[i], cache_b_v[i])
            hs, rs = ab_layer(hs, pos_s, cid_s, kpos_s, kcid_s, past, *w)
            rows_ab_p.append(rp)
            rows_ab_s.append(rs)
        else:
            i = l // 2
            w = (g_norm_c[i], w_in_c[i], g_qk_c[i], w_out_c[i])
            hp, rp = c_layer(hp, pos_p, cid_p, cid_p, None, n_sel_p, *w)
            past = (cache_c_k[i], cache_c_v[i], cache_c_kidx[i])
            hs, rs = c_layer(hs, pos_s, cid_s, kcid_s, past, n_sel_s, *w)
            rows_c_p.append(rp)
            rows_c_s.append(rs)

    def stack_rows(rows, j, trim):
        arr = jnp.stack([r[j] for r in rows], axis=0)
        return arr[:, :, :t_real] if trim else arr

    y_prompt = hp[:, N_META:t_real]
    y_sample = hs
    pa_k = stack_rows(rows_ab_p, 0, True)
    pa_v = stack_rows(rows_ab_p, 1, True)
    pb_k = stack_rows(rows_ab_p, 2, True)
    pb_v = stack_rows(rows_ab_p, 3, True)
    pc_k = stack_rows(rows_c_p, 0, True)
    pc_v = stack_rows(rows_c_p, 1, True)
    pc_i = stack_rows(rows_c_p, 2, True)
    sa_k = stack_rows(rows_ab_s, 0, False)
    sa_v = stack_rows(rows_ab_s, 1, False)
    sb_k = stack_rows(rows_ab_s, 2, False)
    sb_v = stack_rows(rows_ab_s, 3, False)
    sc_k = stack_rows(rows_c_s, 0, False)
    sc_v = stack_rows(rows_c_s, 1, False)
    sc_i = stack_rows(rows_c_s, 2, False)
    return (y_prompt, y_sample, pa_k, pa_v, pb_k, pb_v, pc_k, pc_v, pc_i,
            sa_k, sa_v, sb_k, sb_v, sc_k, sc_v, sc_i)
```

```python
import functools
from typing import NamedTuple

import numpy as np
import jax
import jax.numpy as jnp
from jax import lax
from jax.experimental import pallas as pl
from jax.experimental.pallas import tpu as pltpu

F32 = jnp.float32
BF16 = jnp.bfloat16
I32 = jnp.int32

LANES = 128
TILE = 256
VMEM_LIMIT = 56 * 1024 * 1024

ROPE_THETA = 10000.0
EPS = 1e-6
NEG = -1e30
PAD_CHUNK = 2 ** 30
HEAD_DIM = 64
INT_MIN = -2 ** 31


class Cfg(NamedTuple):
    d_model: int = 1024
    batch: int = 2
    seq: int = 8192
    depth: int = 4
    dec_batch: int = 8
    dec_seq: int = 16
    past_len: int = 2048
    chunk: int = 64
    n_meta: int = 16
    topk_max: int = 256
    a_heads: int = 4
    b_heads: int = 8
    c_heads: int = 16
    idx_heads: int = 4


def _round_up(x, m):
    return -(-x // m) * m


def _log2(n):
    l = int(n).bit_length() - 1
    assert (1 << l) == n, n
    return l


class Geom:
    def __init__(self, cfg, kind):
        self.kind = kind
        self.cfg = cfg
        self.sh = _log2(cfg.chunk)
        if kind == "prompt":
            self.nb = cfg.batch
            self.t_real = cfg.n_meta + cfg.seq
            self.lq = _round_up(self.t_real, TILE)
            self.lk = self.lq
            self.tq = TILE
        else:
            self.nb = cfg.dec_batch
            self.n_keys = cfg.past_len + cfg.dec_seq
            self.lq = LANES
            self.lk = _round_up(self.n_keys, TILE)
            self.tq = LANES
        self.tk = TILE
        self.nq = self.lq // self.tq
        self.nk = self.lk // self.tk

    def qpos(self, i):
        return i if self.kind == "prompt" else i + self.cfg.past_len

    def _cid_prompt(self, p):
        c = self.cfg
        body = lax.shift_right_logical(jnp.maximum(p - c.n_meta, 0), self.sh) + 1
        return jnp.where(p >= self.t_real, PAD_CHUNK, jnp.where(p < c.n_meta, 0, body))

    def qcid(self, pos):
        if self.kind == "prompt":
            return self._cid_prompt(pos)
        return lax.shift_right_logical(pos, self.sh)

    def kcid(self, j):
        if self.kind == "prompt":
            return self._cid_prompt(j)
        return jnp.where(j >= self.n_keys, PAD_CHUNK, lax.shift_right_logical(j, self.sh))

    def kend_cid(self, qpos):
        c = self.cfg
        if self.kind == "prompt":
            end = jnp.minimum(c.n_meta + c.chunk * self._cid_prompt(jnp.minimum(qpos, self.t_real - 1)),
                              self.t_real)
            return jnp.where(qpos >= self.t_real, self.lk, end)
        return jnp.minimum(c.chunk * (lax.shift_right_logical(qpos, self.sh) + 1), self.n_keys)

    def kend_pos(self, qpos):
        return jnp.minimum(qpos, self.lk)

    def tiles(self, q0, rule):
        kend = self.kend_cid if rule == "cid" else self.kend_pos
        first = kend(self.qpos(q0))
        last = kend(self.qpos(q0 + self.tq - 1))
        n_vis = (last + self.tk - 1) // self.tk
        n_full = first // self.tk
        return n_vis, n_full


def _proj_kernel(x_ref, g_ref, w_ref, cos_ref, sin_ref, gq_ref, o_ref, *, groups):
    x = x_ref[...]
    ms = jnp.mean(x * x, axis=-1, keepdims=True)
    h = (x * lax.rsqrt(ms + EPS) * g_ref[...]).astype(BF16)
    cos = cos_ref[...]
    sin_s = sin_ref[...]
    lane = lax.broadcasted_iota(I32, (1, LANES), 1)
    first_half = (lane & (HEAD_DIM - 1)) < (HEAD_DIM // 2)
    r = lax.broadcasted_iota(I32, (LANES, LANES), 0)
    c = lax.broadcasted_iota(I32, (LANES, LANES), 1)
    bd = jnp.where(lax.shift_right_logical(r, 6) == lax.shift_right_logical(c, 6), 1.0, 0.0).astype(BF16)

    def rope(y):
        part = jnp.where(first_half, pltpu.roll(y, LANES - HEAD_DIM // 2, 1), pltpu.roll(y, HEAD_DIM // 2, 1))
        return y * cos + part * sin_s

    for (c0, c1, kind, gi, scale) in groups:
        y = jnp.dot(h, w_ref[:, c0:c1], preferred_element_type=F32)
        if kind == "plain":
            o_ref[:, c0:c1] = y if scale == 1.0 else y * scale
            continue
        for j in range((c1 - c0) // LANES):
            yb = y[:, j * LANES:(j + 1) * LANES]
            if kind == "norm_rope":
                sq = yb * yb
                hi = sq.astype(BF16)
                lo = (sq - hi.astype(F32)).astype(BF16)
                ss = jnp.dot(hi, bd, preferred_element_type=F32) + jnp.dot(lo, bd, preferred_element_type=F32)
                yb = yb * lax.rsqrt(ss * (1.0 / HEAD_DIM) + EPS) * gq_ref[gi:gi + 1, :]
                out = rope(yb)
            elif kind == "rope":
                out = rope(yb)
            else:
                out = jnp.where(lane < HEAD_DIM, rope(yb), jnp.where(lane < HEAD_DIM + 4, yb, 0.0))
            if scale != 1.0:
                out = out * jnp.where(lane < HEAD_DIM, 1.0, scale) if kind == "kiwi" else out * scale
            o_ref[:, c0 + j * LANES:c0 + (j + 1) * LANES] = out


def _proj(x2d, g, w_bf, cos_t, sin_t, gq, groups, tm):
    rows, d = x2d.shape
    n = w_bf.shape[1]
    period = cos_t.shape[0] // tm
    return pl.pallas_call(
        functools.partial(_proj_kernel, groups=groups),
        grid=(rows // tm,),
        in_specs=[
            pl.BlockSpec((tm, d), lambda i: (i, 0)),
            pl.BlockSpec((1, d), lambda i: (0, 0)),
            pl.BlockSpec((d, n), lambda i: (0, 0)),
            pl.BlockSpec((tm, LANES), lambda i: (i % period, 0)),
            pl.BlockSpec((tm, LANES), lambda i: (i % period, 0)),
            pl.BlockSpec(gq.shape, lambda i: (0, 0)),
        ],
        out_specs=pl.BlockSpec((tm, n), lambda i: (i, 0)),
        out_shape=jax.ShapeDtypeStruct((rows, n), F32),
        compiler_params=pltpu.CompilerParams(dimension_semantics=("arbitrary",), vmem_limit_bytes=VMEM_LIMIT),
        name="proj",
    )(x2d, g, w_bf, cos_t, sin_t, gq)


def _half_rows(qT):
    qf = qT.astype(F32)
    row = lax.broadcasted_iota(I32, qf.shape, 0)
    return (jnp.where(row < HEAD_DIM, qf, 0.0).astype(BF16), jnp.where(row >= HEAD_DIM, qf, 0.0).astype(BF16))


def _two_loops(n_full, n_vis, body, descending=False):
    if descending:
        lax.fori_loop(0, n_vis - n_full, lambda i, c: (body(n_vis - 1 - i, True), c)[1], 0)
        lax.fori_loop(0, n_full, lambda i, c: (body(n_full - 1 - i, False), c)[1], 0)
    else:
        lax.fori_loop(0, n_full, lambda i, c: (body(i, False), c)[1], 0)
        lax.fori_loop(n_full, n_vis, lambda i, c: (body(i, True), c)[1], 0)


def _softmax_step(s, v, m_ref, l_ref, acc_ref, idx):
    m_old = m_ref[idx]
    m_new = jnp.maximum(m_old, jnp.max(s, axis=0, keepdims=True))
    alpha = jnp.exp(m_old - m_new)
    p = jnp.exp(s - m_new)
    l_ref[idx] = alpha * l_ref[idx] + jnp.sum(p, axis=0, keepdims=True)
    acc_ref[idx] = alpha * acc_ref[idx] + jnp.dot(v, p.astype(BF16), preferred_element_type=F32)
    m_ref[idx] = m_new


def _attn_a_kernel(lam_ref, q_ref, k_ref, v_ref, o_ref, m_s, l_s, acc_s, *, geom, lam_init):
    tq, tk = geom.tq, geom.tk
    q0 = pl.program_id(2) * tq
    qm = _half_rows(q_ref[...])
    qcid = geom.qcid(geom.qpos(q0 + lax.broadcasted_iota(I32, (1, tq), 1)))
    n_vis, n_full = geom.tiles(q0, "cid")
    m_s[...] = jnp.full(m_s.shape, NEG, F32)
    l_s[...] = jnp.zeros(l_s.shape, F32)
    acc_s[...] = jnp.zeros(acc_s.shape, F32)

    def body(kt, masked):
        k = k_ref[pl.ds(pl.multiple_of(kt * tk, tk), tk), :]
        v = v_ref[kt]
        if masked:
            vis = geom.kcid(kt * tk + lax.broadcasted_iota(I32, (tk, 1), 0)) <= qcid
        for mp in range(2):
            s = jnp.dot(k, qm[mp], preferred_element_type=F32)
            if masked:
                s = jnp.where(vis, s, NEG)
            _softmax_step(s, v, m_s, l_s, acc_s, mp)

    _two_loops(n_full, n_vis, body)
    lp = lam_ref[...]
    lam = (jnp.exp(jnp.sum(lp[0:1] * lp[1:2], axis=-1, keepdims=True))
           - jnp.exp(jnp.sum(lp[2:3] * lp[3:4], axis=-1, keepdims=True)) + lam_init)
    oT = acc_s[0] / l_s[0] - lam * (acc_s[1] / l_s[1])
    o_ref[...] = oT.T


def _attn_a(geom, lam_p, qT, kbf, vT, n_heads, lam_init):
    nb, tq, tk, nk = geom.nb, geom.tq, geom.tk, geom.nk
    return pl.pallas_call(
        functools.partial(_attn_a_kernel, geom=geom, lam_init=lam_init),
        grid=(nb, n_heads, geom.nq),
        in_specs=[
            pl.BlockSpec(lam_p.shape, lambda b, h, i: (0, 0)),
            pl.BlockSpec((None, LANES, tq), lambda b, h, i: (b, h, i)),
            pl.BlockSpec((None, geom.lk, LANES), lambda b, h, i: (b, 0, h)),
            pl.BlockSpec((None, nk, LANES, tk), lambda b, h, i: (b, 0, h, 0)),
        ],
        out_specs=pl.BlockSpec((None, tq, LANES), lambda b, h, i: (b, i, h)),
        out_shape=jax.ShapeDtypeStruct((nb, geom.lq, n_heads * LANES), F32),
        scratch_shapes=[pltpu.VMEM((2, 1, tq), F32), pltpu.VMEM((2, 1, tq), F32), pltpu.VMEM((2, LANES, tq), F32)],
        compiler_params=pltpu.CompilerParams(dimension_semantics=("arbitrary",) * 3, vmem_limit_bytes=VMEM_LIMIT),
        name="attn_a",
    )(lam_p, qT, kbf, vT)


def _attn_b_kernel(q_ref, k_ref, v_ref, o_ref, c_s, acc_s, *, geom):
    tq, tk = geom.tq, geom.tk
    q0 = pl.program_id(2) * tq
    qh = _half_rows(q_ref[...])
    qpos = geom.qpos(q0 + lax.broadcasted_iota(I32, (1, tq), 1))
    n_vis, n_full = geom.tiles(q0, "pos")
    c_s[...] = jnp.zeros(c_s.shape, F32)
    acc_s[...] = jnp.zeros(acc_s.shape, F32)
    r = lax.broadcasted_iota(I32, (tk, tk), 0)
    c = lax.broadcasted_iota(I32, (tk, tk), 1)
    upper = jnp.where(c > r, 1.0, 0.0).astype(BF16)

    def body(kt, masked):
        k = k_ref[pl.ds(pl.multiple_of(kt * tk, tk), tk), :]
        v = v_ref[kt]
        if masked:
            vis = (kt * tk + lax.broadcasted_iota(I32, (tk, 1), 0)) < qpos
        for hh in range(2):
            z = jnp.dot(k, qh[hh], preferred_element_type=F32)
            sp = jnp.maximum(z, 0.0) + jnp.log1p(jnp.exp(-jnp.abs(z)))
            log_keep = -sp
            log_beta = z - sp
            if masked:
                log_keep = jnp.where(vis, log_keep, 0.0)
            hi = log_keep.astype(BF16)
            lo = (log_keep - hi.astype(F32)).astype(BF16)
            after = (jnp.dot(upper, hi, preferred_element_type=F32)
                     + jnp.dot(upper, lo, preferred_element_type=F32) + c_s[hh])
            a = jnp.exp(log_beta + after)
            if masked:
                a = jnp.where(vis, a, 0.0)
            vh = v[hh * HEAD_DIM:(hh + 1) * HEAD_DIM, :]
            acc_s[hh] = acc_s[hh] + jnp.dot(vh, a.astype(BF16), preferred_element_type=F32)
            c_s[hh] = c_s[hh] + jnp.sum(log_keep, axis=0, keepdims=True)

    _two_loops(n_full, n_vis, body, descending=True)
    o_ref[...] = jnp.concatenate([acc_s[0], acc_s[1]], axis=0).T


def _attn_b(geom, qT, kbf, vT, n_pairs, blk0):
    nb, tq, tk, nk = geom.nb, geom.tq, geom.tk, geom.nk
    return pl.pallas_call(
        functools.partial(_attn_b_kernel, geom=geom),
        grid=(nb, n_pairs, geom.nq),
        in_specs=[
            pl.BlockSpec((None, LANES, tq), lambda b, h, i: (b, blk0 + h, i)),
            pl.BlockSpec((None, geom.lk, LANES), lambda b, h, i: (b, 0, blk0 + h)),
            pl.BlockSpec((None, nk, LANES, tk), lambda b, h, i: (b, 0, blk0 + h, 0)),
        ],
        out_specs=pl.BlockSpec((None, tq, LANES), lambda b, h, i: (b, i, h)),
        out_shape=jax.ShapeDtypeStruct((nb, geom.lq, n_pairs * LANES), F32),
        scratch_shapes=[pltpu.VMEM((2, 1, tq), F32), pltpu.VMEM((2, HEAD_DIM, tq), F32)],
        compiler_params=pltpu.CompilerParams(dimension_semantics=("arbitrary",) * 3, vmem_limit_bytes=VMEM_LIMIT),
        name="attn_b",
    )(qT, kbf, vT)


def _select_kernel(qi_ref, wi_ref, ki_ref, mask_ref, keys_s, *, geom, n_sel, n_idx):
    tq, tk = geom.tq, geom.tk
    q0 = pl.program_id(1) * tq
    qcid = geom.qcid(geom.qpos(q0 + lax.broadcasted_iota(I32, (1, tq), 1)))
    n_vis, _ = geom.tiles(q0, "cid")
    zpad = jnp.zeros((LANES - HEAD_DIM, tq), BF16)
    qh = [jnp.concatenate([qi_ref[h * HEAD_DIM:(h + 1) * HEAD_DIM, :], zpad], axis=0) for h in range(n_idx)]
    wi = wi_ref[...]

    def vis_of(kt):
        return geom.kcid(kt * tk + lax.broadcasted_iota(I32, (tk, 1), 0)) <= qcid

    def rows(kt):
        return pl.ds(pl.multiple_of(kt * tk, tk), tk)

    def score_tile(kt, carry):
        ki = ki_ref[rows(kt), :]
        sc = jnp.zeros((tk, tq), F32)
        for h in range(n_idx):
            rel = jnp.maximum(jnp.dot(ki, qh[h], preferred_element_type=F32), 0.0)
            sc = sc + wi[h:h + 1, :] * rel
        sc = jnp.where(sc == 0.0, 0.0, sc)
        sc = jnp.where(vis_of(kt), sc, NEG)
        bits = lax.bitcast_convert_type(sc, I32)
        keys_s[rows(kt), :] = jnp.where(bits < 0, bits ^ jnp.int32(0x7FFFFFFF), bits)
        return carry

    lax.fori_loop(0, n_vis, score_tile, 0)

    def count(pred_fn):
        def step(kt, acc):
            hit = jnp.where(pred_fn(keys_s[rows(kt), :]), 1, 0)
            return acc + jnp.sum(hit.reshape(tk // 8, 8, tq), axis=0)
        acc = lax.fori_loop(0, n_vis, step, jnp.zeros((8, tq), I32))
        return jnp.sum(acc, axis=0, keepdims=True)

    def bit_step(i, prefix):
        cand = prefix | lax.shift_left(jnp.int32(1), 31 - i)
        cand_s = cand ^ jnp.int32(INT_MIN)
        cnt = count(lambda key: key >= cand_s)
        return jnp.where(cnt >= n_sel, cand, prefix)

    prefix = lax.fori_loop(0, 32, bit_step, jnp.zeros((1, tq), I32))
    thr = prefix ^ jnp.int32(INT_MIN)
    need = (n_sel - count(lambda key: key > thr)).astype(F32)

    r = lax.broadcasted_iota(I32, (tk, tk), 0)
    c = lax.broadcasted_iota(I32, (tk, tk), 1)
    lower = jnp.where(c < r, 1.0, 0.0).astype(BF16)

    def mask_tile(kt, run):
        key = keys_s[rows(kt), :]
        eq = jnp.where(key == thr, 1.0, 0.0)
        rank = jnp.dot(lower, eq.astype(BF16), preferred_element_type=F32) + run
        sel = (key > thr) | ((key == thr) & (rank < need))
        sel = sel & vis_of(kt)
        mask_ref[rows(kt), :] = jnp.where(sel, 1, 0).astype(jnp.int8)
        return run + jnp.sum(eq, axis=0, keepdims=True)

    lax.fori_loop(0, n_vis, mask_tile, jnp.zeros((1, tq), F32))

    def zero_tile(kt, carry):
        mask_ref[rows(kt), :] = jnp.zeros((tk, tq), jnp.int8)
        return carry

    lax.fori_loop(n_vis, geom.nk, zero_tile, 0)


def _select(geom, qiT, wiT, ki_bf, n_sel, n_idx):
    nb, tq = geom.nb, geom.tq
    return pl.pallas_call(
        functools.partial(_select_kernel, geom=geom, n_sel=n_sel, n_idx=n_idx),
        grid=(nb, geom.nq),
        in_specs=[
            pl.BlockSpec((None, n_idx * HEAD_DIM, tq), lambda b, i: (b, 0, i)),
            pl.BlockSpec((None, 8, tq), lambda b, i: (b, 0, i)),
            pl.BlockSpec((None, geom.lk, LANES), lambda b, i: (b, 0, 0)),
        ],
        out_specs=pl.BlockSpec((None, geom.lk, tq), lambda b, i: (b, 0, i)),
        out_shape=jax.ShapeDtypeStruct((nb, geom.lk, geom.lq), jnp.int8),
        scratch_shapes=[pltpu.VMEM((geom.lk, tq), I32)],
        compiler_params=pltpu.CompilerParams(dimension_semantics=("arbitrary",) * 2, vmem_limit_bytes=VMEM_LIMIT),
        name="select_c",
    )(qiT, wiT, ki_bf)


def _attn_c_kernel(q_ref, k_ref, v_ref, mask_ref, o_ref, m_s, l_s, acc_s, *, geom):
    tq, tk = geom.tq, geom.tk
    q0 = pl.program_id(2) * tq
    qh = _half_rows(q_ref[...])
    n_vis, _ = geom.tiles(q0, "cid")
    m_s[...] = jnp.full(m_s.shape, NEG, F32)
    l_s[...] = jnp.zeros(l_s.shape, F32)
    acc_s[...] = jnp.zeros(acc_s.shape, F32)

    def body(kt, carry):
        rows = pl.ds(pl.multiple_of(kt * tk, tk), tk)
        k = k_ref[rows, :]
        v = v_ref[kt]
        sel = mask_ref[rows, :].astype(I32) != 0
        for hh in range(2):
            s = jnp.where(sel, jnp.dot(k, qh[hh], preferred_element_type=F32), NEG)
            _softmax_step(s, v[hh * HEAD_DIM:(hh + 1) * HEAD_DIM, :], m_s, l_s, acc_s, hh)
        return carry

    lax.fori_loop(0, n_vis, body, 0)
    o_ref[...] = jnp.concatenate([acc_s[0] / l_s[0], acc_s[1] / l_s[1]], axis=0).T


def _attn_c(geom, qT, kbf, vT, mask, n_pairs):
    nb, tq, tk, nk = geom.nb, geom.tq, geom.tk, geom.nk
    return pl.pallas_call(
        functools.partial(_attn_c_kernel, geom=geom),
        grid=(nb, n_pairs, geom.nq),
        in_specs=[
            pl.BlockSpec((None, LANES, tq), lambda b, h, i: (b, h, i)),
            pl.BlockSpec((None, geom.lk, LANES), lambda b, h, i: (b, 0, h)),
            pl.BlockSpec((None, nk, LANES, tk), lambda b, h, i: (b, 0, h, 0)),
            pl.BlockSpec((None, geom.lk, tq), lambda b, h, i: (b, 0, i)),
        ],
        out_specs=pl.BlockSpec((None, tq, LANES), lambda b, h, i: (b, i, h)),
        out_shape=jax.ShapeDtypeStruct((nb, geom.lq, n_pairs * LANES), F32),
        scratch_shapes=[pltpu.VMEM((2, 1, tq), F32), pltpu.VMEM((2, 1, tq), F32), pltpu.VMEM((2, HEAD_DIM, tq), F32)],
        compiler_params=pltpu.CompilerParams(dimension_semantics=("arbitrary",) * 3, vmem_limit_bytes=VMEM_LIMIT),
        name="attn_c",
    )(qT, kbf, vT, mask)


def _silu(g):
    return g * (1.0 / (1.0 + jnp.exp(-g)))


def _out_ab_kernel(oa_ref, ob_ref, g_ref, x_ref, gsub_ref, w_ref, o_ref, *, n_heads, scale):
    wa = n_heads * LANES
    acc = x_ref[...]
    g = g_ref[...]
    for h in range(n_heads):
        blk = oa_ref[:, h * LANES:(h + 1) * LANES]
        ms = jnp.mean(blk * blk, axis=-1, keepdims=True)
        nrm = (blk * lax.rsqrt(ms + EPS) * gsub_ref[...]) * scale
        mixed = (nrm * _silu(g[:, h * LANES:(h + 1) * LANES])).astype(BF16)
        acc = acc + jnp.dot(mixed, w_ref[h * LANES:(h + 1) * LANES, :], preferred_element_type=F32)
    mixed_b = (ob_ref[...] * _silu(g[:, wa:])).astype(BF16)
    o_ref[...] = acc + jnp.dot(mixed_b, w_ref[wa:, :], preferred_element_type=F32)


def _out_ab(oa, ob, y, gcol, x2d, gsub, w_bf, n_heads, scale, tm):
    rows, d = x2d.shape
    wa, wb = oa.shape[1], ob.shape[1]
    return pl.pallas_call(
        functools.partial(_out_ab_kernel, n_heads=n_heads, scale=scale),
        grid=(rows // tm,),
        in_specs=[
            pl.BlockSpec((tm, wa), lambda i: (i, 0)),
            pl.BlockSpec((tm, wb), lambda i: (i, 0)),
            pl.BlockSpec((tm, wa + wb), lambda i: (i, gcol)),
            pl.BlockSpec((tm, d), lambda i: (i, 0)),
            pl.BlockSpec((1, LANES), lambda i: (0, 0)),
            pl.BlockSpec((wa + wb, d), lambda i: (0, 0)),
        ],
        out_specs=pl.BlockSpec((tm, d), lambda i: (i, 0)),
        out_shape=jax.ShapeDtypeStruct((rows, d), F32),
        compiler_params=pltpu.CompilerParams(dimension_semantics=("arbitrary",), vmem_limit_bytes=VMEM_LIMIT),
        name="out_ab",
    )(oa, ob, y, x2d, gsub, w_bf)


def _out_c_kernel(o_ref_in, g_ref, x_ref, w_ref, o_ref):
    mixed = (o_ref_in[...] * _silu(g_ref[...])).astype(BF16)
    o_ref[...] = x_ref[...] + jnp.dot(mixed, w_ref[...], preferred_element_type=F32)


def _out_c(o, y, gcol, x2d, w_bf, tm):
    rows, d = x2d.shape
    wc = o.shape[1]
    return pl.pallas_call(
        _out_c_kernel,
        grid=(rows // tm,),
        in_specs=[
            pl.BlockSpec((tm, wc), lambda i: (i, 0)),
            pl.BlockSpec((tm, wc), lambda i: (i, gcol)),
            pl.BlockSpec((tm, d), lambda i: (i, 0)),
            pl.BlockSpec((wc, d), lambda i: (0, 0)),
        ],
        out_specs=pl.BlockSpec((tm, d), lambda i: (i, 0)),
        out_shape=jax.ShapeDtypeStruct((rows, d), F32),
        compiler_params=pltpu.CompilerParams(dimension_semantics=("arbitrary",), vmem_limit_bytes=VMEM_LIMIT),
        name="out_c",
    )(o, y, x2d, w_bf)


def _rope_tables(pos):
    half = HEAD_DIM // 2
    inv_freq = ROPE_THETA ** (-jnp.arange(half, dtype=F32) / half)
    ang = pos.astype(F32)[:, None] * inv_freq[None, :]
    cos, sin = jnp.cos(ang), jnp.sin(ang)
    return jnp.tile(cos, (1, 4)), jnp.concatenate([-sin, sin, -sin, sin], axis=1)


def _to_qT(q, geom):
    qT = jnp.swapaxes(q.astype(BF16), 1, 2)
    return jnp.pad(qT, ((0, 0), (0, 0), (0, geom.lq - qT.shape[2])))


def _to_k(k, geom):
    k = k.astype(BF16)
    return jnp.pad(k, ((0, 0), (0, geom.lk - k.shape[1]), (0, 0)))


def _to_vT(v, geom):
    v = _to_k(v, geom)
    nb, _, w = v.shape
    return jnp.swapaxes(v.reshape(nb, geom.nk, geom.tk, w), 2, 3)


def _forward(cfg, x_prompt, x_sample, cache_a_k, cache_a_v, cache_b_k, cache_b_v, cache_c_k, cache_c_v,
             cache_c_kidx, meta_tokens, g_norm_ab, w_in_ab, g_qk_a, lam_a, g_sub_a, w_out_ab,
             g_norm_c, w_in_c, g_qk_c, w_out_c):
    d = cfg.d_model
    gp, gs = Geom(cfg, "prompt"), Geom(cfg, "sample")
    t_real = gp.t_real
    aw = cfg.a_heads * 2 * HEAD_DIM
    bw = cfg.b_heads * HEAD_DIM
    cw = cfg.c_heads * HEAD_DIM
    iw = cfg.idx_heads * HEAD_DIM
    n_sel_p = min(cfg.topk_max, cfg.seq // 4)
    n_sel_s = min(cfg.topk_max, (cfg.past_len + cfg.dec_seq) // 4)

    meta = jnp.broadcast_to(meta_tokens.astype(x_prompt.dtype)[None], (cfg.batch, cfg.n_meta, d))
    hp = jnp.concatenate([meta, x_prompt, jnp.zeros((cfg.batch, gp.lq - t_real, d), x_prompt.dtype)], axis=1)
    hp = hp.reshape(cfg.batch * gp.lq, d)
    rows_s = cfg.dec_batch * cfg.dec_seq
    tm_s = _round_up(rows_s, 8)
    hs = jnp.pad(x_sample.reshape(rows_s, d), ((0, tm_s - rows_s), (0, 0)))

    cos_p, sin_p = _rope_tables(jnp.arange(gp.lq, dtype=I32))
    cos_s, sin_s = _rope_tables(cfg.past_len + jnp.arange(tm_s, dtype=I32) % cfg.dec_seq)

    s_qk = HEAD_DIM ** -0.5
    groups_ab = ((0, aw, "norm_rope", 0, s_qk), (aw, aw + bw, "plain", 0, s_qk),
                 (aw + bw, 2 * aw + bw, "norm_rope", 1, 1.0), (2 * aw + bw, 2 * (aw + bw), "plain", 0, 1.0),
                 (2 * (aw + bw), 3 * (aw + bw), "plain", 0, 1.0), (3 * (aw + bw), 4 * (aw + bw), "plain", 0, 1.0))
    groups_c = tuple([(j * 512, (j + 1) * 512, "norm_rope", 0, s_qk) for j in range(cw // 512)]
                     + [(cw + j * 512, cw + (j + 1) * 512, "norm_rope", 1, 1.0) for j in range(cw // 512)]
                     + [(2 * cw, 3 * cw, "plain", 0, 1.0), (3 * cw, 4 * cw, "plain", 0, 1.0),
                        (4 * cw, 4 * cw + iw, "rope", 0, s_qk),
                        (4 * cw + iw, 4 * cw + iw + LANES, "kiwi", 0, cfg.idx_heads ** -0.5)])
    nc_pad = 4 * cw + iw + LANES

    def split_rows(y, geom, n_rows):
        if geom.kind == "prompt":
            return y.reshape(geom.nb, geom.lq, -1)[:, :n_rows]
        return y[:rows_s].reshape(geom.nb, cfg.dec_seq, -1)

    rows_ab_p, rows_ab_s, rows_c_p, rows_c_s = [], [], [], []
    for l in range(cfg.depth):
        i = l // 2
        if l % 2 == 0:
            lam_init = 0.8 - 0.6 * float(np.exp(-0.3 * l))
            w = w_in_ab[i]
            wq = [w[:, j * aw:(j + 1) * aw] for j in range(4)] + [w[:, 4 * aw + j * bw:4 * aw + (j + 1) * bw] for j in range(4)]
            w_perm = jnp.concatenate([wq[0], wq[4], wq[1], wq[5], wq[2], wq[6], wq[3], wq[7]], axis=1).astype(BF16)
            gq = jnp.tile(g_qk_a[i], (1, 2))
            gq = jnp.pad(gq, ((0, 8 - gq.shape[0]), (0, 0)))
            w_out = w_out_ab[i].astype(BF16)
            gsub = g_sub_a[i][None, :]
            gn = g_norm_ab[i][None, :]
            qk_w, v_off, g_off = aw + bw, 2 * (aw + bw), 3 * (aw + bw)
            for geom, which in ((gp, "p"), (gs, "s")):
                x2d, cos_t, sin_t, tm = (hp, cos_p, sin_p, TILE) if which == "p" else (hs, cos_s, sin_s, tm_s)
                y = _proj(x2d, gn, w_perm, cos_t, sin_t, gq, groups_ab, tm)
                if which == "p":
                    y3 = split_rows(y, geom, geom.lq)
                    new = y3[:, :t_real]
                    qT = _to_qT(y3[..., 0:qk_w], geom)
                    kbf = _to_k(y3[..., qk_w:2 * qk_w], geom)
                    vT = _to_vT(y3[..., v_off:v_off + qk_w], geom)
                else:
                    new = split_rows(y, geom, cfg.dec_seq)
                    qT = _to_qT(new[..., 0:qk_w], geom)
                    past_k = jnp.concatenate([cache_a_k[i].reshape(geom.nb, cfg.past_len, aw),
                                              cache_b_k[i].reshape(geom.nb, cfg.past_len, bw)], axis=-1)
                    past_v = jnp.concatenate([cache_a_v[i].reshape(geom.nb, cfg.past_len, aw),
                                              cache_b_v[i].reshape(geom.nb, cfg.past_len, bw)], axis=-1)
                    kbf = _to_k(jnp.concatenate([past_k, new[..., qk_w:2 * qk_w]], axis=1), geom)
                    vT = _to_vT(jnp.concatenate([past_v, new[..., v_off:v_off + qk_w]], axis=1), geom)
                n_rows = new.shape[1]
                rows = (new[..., qk_w:qk_w + aw].reshape(geom.nb, n_rows, cfg.a_heads, 2 * HEAD_DIM),
                        new[..., v_off:v_off + aw].reshape(geom.nb, n_rows, cfg.a_heads, 2 * HEAD_DIM),
                        new[..., qk_w + aw:2 * qk_w].reshape(geom.nb, n_rows, cfg.b_heads, HEAD_DIM),
                        new[..., v_off + aw:v_off + qk_w].reshape(geom.nb, n_rows, cfg.b_heads, HEAD_DIM))
                (rows_ab_p if which == "p" else rows_ab_s).append(rows)
                oa = _attn_a(geom, lam_a[i], qT, kbf, vT, cfg.a_heads, lam_init)
                ob = _attn_b(geom, qT, kbf, vT, cfg.b_heads // 2, cfg.a_heads)
                if which == "p":
                    hp = _out_ab(oa.reshape(-1, aw), ob.reshape(-1, bw), y, g_off // (aw + bw), hp, gsub, w_out,
                                 cfg.a_heads, 1.0 - lam_init, TILE)
                else:
                    oa_s = jnp.pad(oa[:, :cfg.dec_seq].reshape(rows_s, aw), ((0, tm_s - rows_s), (0, 0)))
                    ob_s = jnp.pad(ob[:, :cfg.dec_seq].reshape(rows_s, bw), ((0, tm_s - rows_s), (0, 0)))
                    hs = _out_ab(oa_s, ob_s, y, g_off // (aw + bw), hs, gsub, w_out, cfg.a_heads, 1.0 - lam_init, tm_s)
        else:
            w = w_in_c[i]
            w_pad = jnp.pad(w, ((0, 0), (0, nc_pad - w.shape[1]))).astype(BF16)
            gq = jnp.tile(g_qk_c[i], (1, 2))
            gq = jnp.pad(gq, ((0, 8 - gq.shape[0]), (0, 0)))
            w_out = w_out_c[i].astype(BF16)
            gn = g_norm_c[i][None, :]
            for geom, which in ((gp, "p"), (gs, "s")):
                x2d, cos_t, sin_t, tm = (hp, cos_p, sin_p, TILE) if which == "p" else (hs, cos_s, sin_s, tm_s)
                y = _proj(x2d, gn, w_pad, cos_t, sin_t, gq, groups_c, tm)
                if which == "p":
                    y3 = split_rows(y, geom, geom.lq)
                    new = y3[:, :t_real]
                    n_sel = n_sel_p
                    k_all, v_all = y3[..., cw:2 * cw], y3[..., 2 * cw:3 * cw]
                    ki_all = y3[..., 4 * cw + iw:4 * cw + iw + LANES]
                    q_new, qi_new, wi_new = y3[..., 0:cw], y3[..., 4 * cw:4 * cw + iw], ki_all[..., HEAD_DIM:HEAD_DIM + 8]
                else:
                    new = split_rows(y, geom, cfg.dec_seq)
                    n_sel = n_sel_s
                    k_all = jnp.concatenate([cache_c_k[i].reshape(geom.nb, cfg.past_len, cw), new[..., cw:2 * cw]], axis=1)
                    v_all = jnp.concatenate([cache_c_v[i].reshape(geom.nb, cfg.past_len, cw), new[..., 2 * cw:3 * cw]], axis=1)
                    ki_new = new[..., 4 * cw + iw:4 * cw + iw + LANES]
                    ki_past = jnp.pad(cache_c_kidx[i], ((0, 0), (0, 0), (0, LANES - HEAD_DIM)))
                    ki_all = jnp.concatenate([ki_past, jnp.where(jnp.arange(LANES) < HEAD_DIM, ki_new, 0.0)], axis=1)
                    q_new, qi_new, wi_new = new[..., 0:cw], new[..., 4 * cw:4 * cw + iw], ki_new[..., HEAD_DIM:HEAD_DIM + 8]
                n_rows = new.shape[1]
                rows = (new[..., cw:2 * cw].reshape(geom.nb, n_rows, cfg.c_heads, HEAD_DIM),
                        new[..., 2 * cw:3 * cw].reshape(geom.nb, n_rows, cfg.c_heads, HEAD_DIM),
                        new[..., 4 * cw + iw:4 * cw + iw + HEAD_DIM])
                (rows_c_p if which == "p" else rows_c_s).append(rows)
                qT = _to_qT(q_new, geom)
                kbf = _to_k(k_all, geom)
                vT = _to_vT(v_all, geom)
                qiT = _to_qT(qi_new, geom)
                wiT = jnp.pad(jnp.swapaxes(wi_new, 1, 2), ((0, 0), (0, 0), (0, geom.lq - wi_new.shape[1])))
                ki_bf = _to_k(jnp.where(jnp.arange(LANES) < HEAD_DIM, ki_all, 0.0), geom)
                mask = _select(geom, qiT, wiT, ki_bf, n_sel, cfg.idx_heads)
                oc = _attn_c(geom, qT, kbf, vT, mask, cfg.c_heads // 2)
                if which == "p":
                    hp = _out_c(oc.reshape(-1, cw), y, 3, hp, w_out, TILE)
                else:
                    oc_s = jnp.pad(oc[:, :cfg.dec_seq].reshape(rows_s, cw), ((0, tm_s - rows_s), (0, 0)))
                    hs = _out_c(oc_s, y, 3, hs, w_out, tm_s)

    y_prompt = hp.reshape(cfg.batch, gp.lq, d)[:, cfg.n_meta:t_real]
    y_sample = hs[:rows_s].reshape(cfg.dec_batch, cfg.dec_seq, d)

    def stack(rows, j):
        return jnp.stack([r[j] for r in rows], axis=0)

    return (y_prompt, y_sample,
            stack(rows_ab_p, 0), stack(rows_ab_p, 1), stack(rows_ab_p, 2), stack(rows_ab_p, 3),
            stack(rows_c_p, 0), stack(rows_c_p, 1), stack(rows_c_p, 2),
            stack(rows_ab_s, 0), stack(rows_ab_s, 1), stack(rows_ab_s, 2), stack(rows_ab_s, 3),
            stack(rows_c_s, 0), stack(rows_c_s, 1), stack(rows_c_s, 2))


def kernel(x_prompt, x_sample, cache_a_k, cache_a_v, cache_b_k, cache_b_v, cache_c_k, cache_c_v, cache_c_kidx,
           meta_tokens, g_norm_ab, w_in_ab, g_qk_a, lam_a, g_sub_a, w_out_ab, g_norm_c, w_in_c, g_qk_c, w_out_c):
    return _forward(Cfg(), x_prompt, x_sample, cache_a_k, cache_a_v, cache_b_k, cache_b_v, cache_c_k, cache_c_v,
                    cache_c_kidx, meta_tokens, g_norm_ab, w_in_ab, g_qk_a, lam_a, g_sub_a, w_out_ab,
                    g_norm_c, w_in_c, g_qk_c, w_out_c)
```

```python
import functools
from typing import NamedTuple

import numpy as np
import jax
import jax.numpy as jnp
from jax import lax
from jax.experimental import pallas as pl
from jax.experimental.pallas import tpu as pltpu

F32 = jnp.float32
BF16 = jnp.bfloat16
I32 = jnp.int32

LANES = 128
TILE = 256
VMEM_LIMIT = 56 * 1024 * 1024

ROPE_THETA = 10000.0
EPS = 1e-6
NEG = -1e30
PAD_CHUNK = 2 ** 30
HEAD_DIM = 64
INT_MIN = -2 ** 31
LOG2E = 1.4426950408889634
ONES_ROWS = 16
EXP_ZERO = -104.0


class Cfg(NamedTuple):
    d_model: int = 1024
    batch: int = 2
    seq: int = 8192
    depth: int = 4
    dec_batch: int = 8
    dec_seq: int = 16
    past_len: int = 2048
    chunk: int = 64
    n_meta: int = 16
    topk_max: int = 256
    a_heads: int = 4
    b_heads: int = 8
    c_heads: int = 16
    idx_heads: int = 4


def _round_up(x, m):
    return -(-x // m) * m


def _log2(n):
    l = int(n).bit_length() - 1
    assert (1 << l) == n, n
    return l


class Geom:
    def __init__(self, cfg, kind):
        self.kind = kind
        self.cfg = cfg
        self.sh = _log2(cfg.chunk)
        if kind == "prompt":
            self.nb = cfg.batch
            self.t_real = cfg.n_meta + cfg.seq
            self.lq = _round_up(self.t_real, TILE)
            self.lk = _round_up(self.t_real, 2 * TILE)
            self.tq = TILE
        else:
            self.nb = cfg.dec_batch
            self.n_keys = cfg.past_len + cfg.dec_seq
            self.lq = LANES
            self.lk = _round_up(self.n_keys, 2 * TILE)
            self.tq = LANES
        self.tk = TILE
        self.nq = self.lq // self.tq
        self.nk = self.lk // self.tk

    def qpos(self, i):
        return i if self.kind == "prompt" else i + self.cfg.past_len

    def _cid_prompt(self, p):
        c = self.cfg
        body = lax.shift_right_logical(jnp.maximum(p - c.n_meta, 0), self.sh) + 1
        return jnp.where(p >= self.t_real, PAD_CHUNK, jnp.where(p < c.n_meta, 0, body))

    def qcid(self, pos):
        if self.kind == "prompt":
            return self._cid_prompt(pos)
        return lax.shift_right_logical(pos, self.sh)

    def kcid(self, j):
        if self.kind == "prompt":
            return self._cid_prompt(j)
        return jnp.where(j >= self.n_keys, PAD_CHUNK, lax.shift_right_logical(j, self.sh))

    def kend_cid(self, qpos):
        c = self.cfg
        if self.kind == "prompt":
            end = jnp.minimum(c.n_meta + c.chunk * self._cid_prompt(jnp.minimum(qpos, self.t_real - 1)),
                              self.t_real)
            return jnp.where(qpos >= self.t_real, self.lk, end)
        return jnp.minimum(c.chunk * (lax.shift_right_logical(qpos, self.sh) + 1), self.n_keys)

    def kend_pos(self, qpos):
        return jnp.minimum(qpos, self.lk)

    def tiles(self, q0, rule):
        kend = self.kend_cid if rule == "cid" else self.kend_pos
        first = kend(self.qpos(q0))
        last = kend(self.qpos(q0 + self.tq - 1))
        n_vis = (last + self.tk - 1) // self.tk
        n_full = first // self.tk
        return n_vis, n_full


def _proj_kernel(x_ref, g_ref, w_ref, cos_ref, sin_ref, gq_ref, o_ref, *, groups):
    x = x_ref[...]
    ms = jnp.mean(x * x, axis=-1, keepdims=True)
    h = (x * lax.rsqrt(ms + EPS) * g_ref[...]).astype(BF16)
    cos = cos_ref[...]
    sin_s = sin_ref[...]
    lane = lax.broadcasted_iota(I32, (1, LANES), 1)
    first_half = (lane & (HEAD_DIM - 1)) < (HEAD_DIM // 2)
    r = lax.broadcasted_iota(I32, (LANES, LANES), 0)
    c = lax.broadcasted_iota(I32, (LANES, LANES), 1)
    bd = jnp.where(lax.shift_right_logical(r, 6) == lax.shift_right_logical(c, 6), 1.0, 0.0).astype(BF16)

    def rope(y):
        part = jnp.where(first_half, pltpu.roll(y, LANES - HEAD_DIM // 2, 1), pltpu.roll(y, HEAD_DIM // 2, 1))
        return y * cos + part * sin_s

    for (c0, c1, kind, gi, scale) in groups:
        y = jnp.dot(h, w_ref[:, c0:c1], preferred_element_type=F32)
        if kind == "plain":
            o_ref[:, c0:c1] = y if scale == 1.0 else y * scale
            continue
        for j in range((c1 - c0) // LANES):
            yb = y[:, j * LANES:(j + 1) * LANES]
            if kind == "norm_rope":
                sq = yb * yb
                hi = sq.astype(BF16)
                lo = (sq - hi.astype(F32)).astype(BF16)
                ss = jnp.dot(hi, bd, preferred_element_type=F32) + jnp.dot(lo, bd, preferred_element_type=F32)
                yb = yb * lax.rsqrt(ss * (1.0 / HEAD_DIM) + EPS) * gq_ref[gi:gi + 1, :]
                out = rope(yb)
            elif kind == "rope":
                out = rope(yb)
            else:
                out = jnp.where(lane < HEAD_DIM, rope(yb), jnp.where(lane < HEAD_DIM + 4, yb, 0.0))
            if scale != 1.0:
                out = out * jnp.where(lane < HEAD_DIM, 1.0, scale) if kind == "kiwi" else out * scale
            o_ref[:, c0 + j * LANES:c0 + (j + 1) * LANES] = out


def _proj(x2d, g, w_bf, cos_t, sin_t, gq, groups, tm):
    rows, d = x2d.shape
    n = w_bf.shape[1]
    period = cos_t.shape[0] // tm
    return pl.pallas_call(
        functools.partial(_proj_kernel, groups=groups),
        grid=(rows // tm,),
        in_specs=[
            pl.BlockSpec((tm, d), lambda i: (i, 0)),
            pl.BlockSpec((1, d), lambda i: (0, 0)),
            pl.BlockSpec((d, n), lambda i: (0, 0)),
            pl.BlockSpec((tm, LANES), lambda i: (i % period, 0)),
            pl.BlockSpec((tm, LANES), lambda i: (i % period, 0)),
            pl.BlockSpec(gq.shape, lambda i: (0, 0)),
        ],
        out_specs=pl.BlockSpec((tm, n), lambda i: (i, 0)),
        out_shape=jax.ShapeDtypeStruct((rows, n), F32),
        compiler_params=pltpu.CompilerParams(dimension_semantics=("arbitrary",), vmem_limit_bytes=VMEM_LIMIT),
        name="proj",
    )(x2d, g, w_bf, cos_t, sin_t, gq)


def _half_rows(qT):
    qf = qT.astype(F32)
    row = lax.broadcasted_iota(I32, qf.shape, 0)
    return (jnp.where(row < HEAD_DIM, qf, 0.0).astype(BF16), jnp.where(row >= HEAD_DIM, qf, 0.0).astype(BF16))


def _pipelined_softmax(nk, n_full, n_vis, qk, softmax, pv):
    def step(kt, slot, masked):
        qk(jnp.minimum(kt + 1, nk - 1), 1 - slot)
        pv(jnp.maximum(kt - 1, 0), 1 - slot)
        softmax(kt, slot, masked)

    def pair(masked):
        def body(j, carry):
            step(2 * j, 0, masked)
            step(2 * j + 1, 1, masked)
            return carry
        return body

    n_pairs = (n_vis + 1) // 2
    n_full_pairs = n_full // 2
    qk(0, 0)
    lax.fori_loop(0, n_full_pairs, pair(False), 0)
    lax.fori_loop(n_full_pairs, n_pairs, pair(True), 0)
    pv(2 * n_pairs - 1, 1)


def _softmax_tile(s, m_ref, al_ref, p_ref, slot, idx):
    m_old = m_ref[idx]
    m_new = jnp.maximum(m_old, jnp.max(s, axis=0, keepdims=True))
    al_ref[idx] = jnp.exp2(m_old - m_new)
    p_ref[slot, idx] = jnp.exp2(s - m_new).astype(BF16)
    m_ref[idx] = m_new


def _init_pipeline(m_s, acc_s, p_buf, al_s):
    m_s[...] = jnp.full(m_s.shape, NEG, F32)
    acc_s[...] = jnp.zeros(acc_s.shape, F32)
    p_buf[1] = jnp.zeros(p_buf.shape[1:], BF16)
    al_s[...] = jnp.ones(al_s.shape, F32)


def _tile_rows(kt, tk):
    return pl.ds(pl.multiple_of(kt * tk, tk), tk)


def _attn_a_kernel(lam_ref, q_ref, k_ref, v_ref, o_ref, m_s, acc_s, s_buf, p_buf, al_s, *, geom, lam_init):
    tq, tk = geom.tq, geom.tk
    dv = 2 * HEAD_DIM
    q0 = pl.program_id(2) * tq
    qm = _half_rows(q_ref[...])
    qcid = geom.qcid(geom.qpos(q0 + lax.broadcasted_iota(I32, (1, tq), 1)))
    n_vis, n_full = geom.tiles(q0, "cid")
    _init_pipeline(m_s, acc_s, p_buf, al_s)
    ones = jnp.ones((ONES_ROWS, tk), BF16)

    def qk(kt, slot):
        k = k_ref[_tile_rows(kt, tk), :]
        for mp in range(2):
            s_buf[slot, mp] = jnp.dot(k, qm[mp], preferred_element_type=F32)

    def pv(kt, slot):
        va = jnp.concatenate([v_ref[kt], ones], axis=0)
        for mp in range(2):
            acc_s[mp] = al_s[mp] * acc_s[mp] + jnp.dot(va, p_buf[slot, mp], preferred_element_type=F32)

    def softmax(kt, slot, masked):
        if masked:
            vis = geom.kcid(kt * tk + lax.broadcasted_iota(I32, (tk, 1), 0)) <= qcid
        for mp in range(2):
            s = s_buf[slot, mp]
            if masked:
                s = jnp.where(vis, s, NEG)
            _softmax_tile(s, m_s, al_s, p_buf, slot, mp)

    _pipelined_softmax(geom.nk, n_full, n_vis, qk, softmax, pv)
    lp = lam_ref[...]
    lam = (jnp.exp(jnp.sum(lp[0:1] * lp[1:2], axis=-1, keepdims=True))
           - jnp.exp(jnp.sum(lp[2:3] * lp[3:4], axis=-1, keepdims=True)) + lam_init)
    oT = acc_s[0, :dv] / acc_s[0, dv:dv + 1] - lam * (acc_s[1, :dv] / acc_s[1, dv:dv + 1])
    o_ref[...] = oT.T


def _attn_a(geom, lam_p, qT, kbf, vT, n_heads, lam_init):
    nb, tq, tk, nk = geom.nb, geom.tq, geom.tk, geom.nk
    return pl.pallas_call(
        functools.partial(_attn_a_kernel, geom=geom, lam_init=lam_init),
        grid=(nb, n_heads, geom.nq),
        in_specs=[
            pl.BlockSpec(lam_p.shape, lambda b, h, i: (0, 0)),
            pl.BlockSpec((None, LANES, tq), lambda b, h, i: (b, h, i)),
            pl.BlockSpec((None, geom.lk, LANES), lambda b, h, i: (b, 0, h)),
            pl.BlockSpec((None, nk, LANES, tk), lambda b, h, i: (b, 0, h, 0)),
        ],
        out_specs=pl.BlockSpec((None, tq, LANES), lambda b, h, i: (b, i, h)),
        out_shape=jax.ShapeDtypeStruct((nb, geom.lq, n_heads * LANES), F32),
        scratch_shapes=[pltpu.VMEM((2, 1, tq), F32), pltpu.VMEM((2, LANES + ONES_ROWS, tq), F32),
                        pltpu.VMEM((2, 2, tk, tq), F32), pltpu.VMEM((2, 2, tk, tq), BF16), pltpu.VMEM((2, 1, tq), F32)],
        compiler_params=pltpu.CompilerParams(dimension_semantics=("arbitrary",) * 3, vmem_limit_bytes=VMEM_LIMIT),
        name="attn_a",
    )(lam_p, qT, kbf, vT)


def _attn_b_kernel(q_ref, k_ref, v_ref, o_ref, c_s, acc_s, *, geom):
    tq, tk = geom.tq, geom.tk
    q0 = pl.program_id(2) * tq
    qh = _half_rows(q_ref[...])
    qpos = geom.qpos(q0 + lax.broadcasted_iota(I32, (1, tq), 1))
    n_vis, n_full = geom.tiles(q0, "pos")
    c_s[...] = jnp.zeros(c_s.shape, F32)
    acc_s[...] = jnp.zeros(acc_s.shape, F32)
    r = lax.broadcasted_iota(I32, (tk, tk), 0)
    c = lax.broadcasted_iota(I32, (tk, tk), 1)
    upper = jnp.where(c > r, 1.0, 0.0).astype(BF16)

    def body(kt, masked):
        k = k_ref[_tile_rows(kt, tk), :]
        v = v_ref[kt]
        if masked:
            vis = (kt * tk + lax.broadcasted_iota(I32, (tk, 1), 0)) < qpos
        for hh in range(2):
            z = jnp.dot(k, qh[hh], preferred_element_type=F32)
            sp = jnp.maximum(z, 0.0) + jnp.log1p(jnp.exp(-jnp.abs(z)))
            log_keep = -sp
            log_beta = z - sp
            if masked:
                log_keep = jnp.where(vis, log_keep, 0.0)
            hi = log_keep.astype(BF16)
            lo = (log_keep - hi.astype(F32)).astype(BF16)
            after = (jnp.dot(upper, hi, preferred_element_type=F32)
                     + jnp.dot(upper, lo, preferred_element_type=F32) + c_s[hh])
            a = jnp.exp(log_beta + after)
            if masked:
                a = jnp.where(vis, a, 0.0)
            vh = v[hh * HEAD_DIM:(hh + 1) * HEAD_DIM, :]
            acc_s[hh] = acc_s[hh] + jnp.dot(vh, a.astype(BF16), preferred_element_type=F32)
            c_s[hh] = c_s[hh] + jnp.sum(log_keep, axis=0, keepdims=True)

    def alive():
        return jnp.max(c_s[...]) > EXP_ZERO

    lax.fori_loop(0, n_vis - n_full, lambda i, carry: (body(n_vis - 1 - i, True), carry)[1], 0)

    def full_tile(state):
        i, _ = state
        body(n_full - 1 - i, False)
        return i + 1, alive()

    lax.while_loop(lambda state: (state[0] < n_full) & state[1], full_tile, (jnp.int32(0), alive()))
    o_ref[...] = jnp.concatenate([acc_s[0], acc_s[1]], axis=0).T


def _attn_b(geom, qT, kbf, vT, n_pairs, blk0):
    nb, tq, tk, nk = geom.nb, geom.tq, geom.tk, geom.nk
    return pl.pallas_call(
        functools.partial(_attn_b_kernel, geom=geom),
        grid=(nb, n_pairs, geom.nq),
        in_specs=[
            pl.BlockSpec((None, LANES, tq), lambda b, h, i: (b, blk0 + h, i)),
            pl.BlockSpec((None, geom.lk, LANES), lambda b, h, i: (b, 0, blk0 + h)),
            pl.BlockSpec((None, nk, LANES, tk), lambda b, h, i: (b, 0, blk0 + h, 0)),
        ],
        out_specs=pl.BlockSpec((None, tq, LANES), lambda b, h, i: (b, i, h)),
        out_shape=jax.ShapeDtypeStruct((nb, geom.lq, n_pairs * LANES), F32),
        scratch_shapes=[pltpu.VMEM((2, 1, tq), F32), pltpu.VMEM((2, HEAD_DIM, tq), F32)],
        compiler_params=pltpu.CompilerParams(dimension_semantics=("arbitrary",) * 3, vmem_limit_bytes=VMEM_LIMIT),
        name="attn_b",
    )(qT, kbf, vT)


def _select_kernel(qi_ref, wi_ref, ki_ref, mask_ref, keys_s, *, geom, n_sel, n_idx):
    tq, tk = geom.tq, geom.tk
    q0 = pl.program_id(1) * tq
    qcid = geom.qcid(geom.qpos(q0 + lax.broadcasted_iota(I32, (1, tq), 1)))
    n_vis, _ = geom.tiles(q0, "cid")
    zpad = jnp.zeros((LANES - HEAD_DIM, tq), BF16)
    qh = [jnp.concatenate([qi_ref[h * HEAD_DIM:(h + 1) * HEAD_DIM, :], zpad], axis=0) for h in range(n_idx)]
    wi = wi_ref[...]

    def vis_of(kt):
        return geom.kcid(kt * tk + lax.broadcasted_iota(I32, (tk, 1), 0)) <= qcid

    def rows(kt):
        return pl.ds(pl.multiple_of(kt * tk, tk), tk)

    def score_tile(kt, carry):
        ki = ki_ref[rows(kt), :]
        sc = jnp.zeros((tk, tq), F32)
        for h in range(n_idx):
            rel = jnp.maximum(jnp.dot(ki, qh[h], preferred_element_type=F32), 0.0)
            sc = sc + wi[h:h + 1, :] * rel
        sc = jnp.where(sc == 0.0, 0.0, sc)
        sc = jnp.where(vis_of(kt), sc, NEG)
        bits = lax.bitcast_convert_type(sc, I32)
        keys_s[rows(kt), :] = jnp.where(bits < 0, bits ^ jnp.int32(0x7FFFFFFF), bits)
        return carry

    lax.fori_loop(0, n_vis, score_tile, 0)

    def count(pred_fn):
        def step(kt, acc):
            hit = jnp.where(pred_fn(keys_s[rows(kt), :]), 1, 0)
            return acc + jnp.sum(hit.reshape(tk // 8, 8, tq), axis=0)
        acc = lax.fori_loop(0, n_vis, step, jnp.zeros((8, tq), I32))
        return jnp.sum(acc, axis=0, keepdims=True)

    def bit_step(i, prefix):
        cand = prefix | lax.shift_left(jnp.int32(1), 31 - i)
        cand_s = cand ^ jnp.int32(INT_MIN)
        cnt = count(lambda key: key >= cand_s)
        return jnp.where(cnt >= n_sel, cand, prefix)

    prefix = lax.fori_loop(0, 32, bit_step, jnp.zeros((1, tq), I32))
    thr = prefix ^ jnp.int32(INT_MIN)
    need = (n_sel - count(lambda key: key > thr)).astype(F32)

    r = lax.broadcasted_iota(I32, (tk, tk), 0)
    c = lax.broadcasted_iota(I32, (tk, tk), 1)
    lower = jnp.where(c < r, 1.0, 0.0).astype(BF16)

    def mask_tile(kt, run):
        key = keys_s[rows(kt), :]
        eq = jnp.where(key == thr, 1.0, 0.0)
        rank = jnp.dot(lower, eq.astype(BF16), preferred_element_type=F32) + run
        sel = (key > thr) | ((key == thr) & (rank < need))
        sel = sel & vis_of(kt)
        mask_ref[rows(kt), :] = jnp.where(sel, 1, 0).astype(jnp.int8)
        return run + jnp.sum(eq, axis=0, keepdims=True)

    lax.fori_loop(0, n_vis, mask_tile, jnp.zeros((1, tq), F32))

    def zero_tile(kt, carry):
        mask_ref[rows(kt), :] = jnp.zeros((tk, tq), jnp.int8)
        return carry

    lax.fori_loop(n_vis, geom.nk, zero_tile, 0)


def _select(geom, qiT, wiT, ki_bf, n_sel, n_idx):
    nb, tq = geom.nb, geom.tq
    return pl.pallas_call(
        functools.partial(_select_kernel, geom=geom, n_sel=n_sel, n_idx=n_idx),
        grid=(nb, geom.nq),
        in_specs=[
            pl.BlockSpec((None, n_idx * HEAD_DIM, tq), lambda b, i: (b, 0, i)),
            pl.BlockSpec((None, 8, tq), lambda b, i: (b, 0, i)),
            pl.BlockSpec((None, geom.lk, LANES), lambda b, i: (b, 0, 0)),
        ],
        out_specs=pl.BlockSpec((None, geom.lk, tq), lambda b, i: (b, 0, i)),
        out_shape=jax.ShapeDtypeStruct((nb, geom.lk, geom.lq), jnp.int8),
        scratch_shapes=[pltpu.VMEM((geom.lk, tq), I32)],
        compiler_params=pltpu.CompilerParams(dimension_semantics=("arbitrary",) * 2, vmem_limit_bytes=VMEM_LIMIT),
        name="select_c",
    )(qiT, wiT, ki_bf)


def _attn_c_kernel(q_ref, k_ref, v_ref, mask_ref, o_ref, m_s, acc_s, s_buf, p_buf, al_s, *, geom):
    tq, tk = geom.tq, geom.tk
    q0 = pl.program_id(2) * tq
    qh = _half_rows(q_ref[...])
    n_vis, _ = geom.tiles(q0, "cid")
    _init_pipeline(m_s, acc_s, p_buf, al_s)
    ones = jnp.ones((ONES_ROWS, tk), BF16)

    def qk(kt, slot):
        k = k_ref[_tile_rows(kt, tk), :]
        for hh in range(2):
            s_buf[slot, hh] = jnp.dot(k, qh[hh], preferred_element_type=F32)

    def pv(kt, slot):
        v = v_ref[kt]
        for hh in range(2):
            va = jnp.concatenate([v[hh * HEAD_DIM:(hh + 1) * HEAD_DIM, :], ones], axis=0)
            acc_s[hh] = al_s[hh] * acc_s[hh] + jnp.dot(va, p_buf[slot, hh], preferred_element_type=F32)

    def softmax(kt, slot, masked):
        sel = mask_ref[_tile_rows(kt, tk), :].astype(I32) != 0
        for hh in range(2):
            _softmax_tile(jnp.where(sel, s_buf[slot, hh], NEG), m_s, al_s, p_buf, slot, hh)

    _pipelined_softmax(geom.nk, 0, n_vis, qk, softmax, pv)
    o_ref[...] = jnp.concatenate([acc_s[0, :HEAD_DIM] / acc_s[0, HEAD_DIM:HEAD_DIM + 1],
                                  acc_s[1, :HEAD_DIM] / acc_s[1, HEAD_DIM:HEAD_DIM + 1]], axis=0).T


def _attn_c(geom, qT, kbf, vT, mask, n_pairs):
    nb, tq, tk, nk = geom.nb, geom.tq, geom.tk, geom.nk
    return pl.pallas_call(
        functools.partial(_attn_c_kernel, geom=geom),
        grid=(nb, n_pairs, geom.nq),
        in_specs=[
            pl.BlockSpec((None, LANES, tq), lambda b, h, i: (b, h, i)),
            pl.BlockSpec((None, geom.lk, LANES), lambda b, h, i: (b, 0, h)),
            pl.BlockSpec((None, nk, LANES, tk), lambda b, h, i: (b, 0, h, 0)),
            pl.BlockSpec((None, geom.lk, tq), lambda b, h, i: (b, 0, i)),
        ],
        out_specs=pl.BlockSpec((None, tq, LANES), lambda b, h, i: (b, i, h)),
        out_shape=jax.ShapeDtypeStruct((nb, geom.lq, n_pairs * LANES), F32),
        scratch_shapes=[pltpu.VMEM((2, 1, tq), F32), pltpu.VMEM((2, HEAD_DIM + ONES_ROWS, tq), F32),
                        pltpu.VMEM((2, 2, tk, tq), F32), pltpu.VMEM((2, 2, tk, tq), BF16), pltpu.VMEM((2, 1, tq), F32)],
        compiler_params=pltpu.CompilerParams(dimension_semantics=("arbitrary",) * 3, vmem_limit_bytes=VMEM_LIMIT),
        name="attn_c",
    )(qT, kbf, vT, mask)


def _silu(g):
    return g * (1.0 / (1.0 + jnp.exp(-g)))


def _out_ab_kernel(oa_ref, ob_ref, g_ref, x_ref, gsub_ref, w_ref, o_ref, *, n_heads, scale):
    wa = n_heads * LANES
    acc = x_ref[...]
    g = g_ref[...]
    for h in range(n_heads):
        blk = oa_ref[:, h * LANES:(h + 1) * LANES]
        ms = jnp.mean(blk * blk, axis=-1, keepdims=True)
        nrm = (blk * lax.rsqrt(ms + EPS) * gsub_ref[...]) * scale
        mixed = (nrm * _silu(g[:, h * LANES:(h + 1) * LANES])).astype(BF16)
        acc = acc + jnp.dot(mixed, w_ref[h * LANES:(h + 1) * LANES, :], preferred_element_type=F32)
    mixed_b = (ob_ref[...] * _silu(g[:, wa:])).astype(BF16)
    o_ref[...] = acc + jnp.dot(mixed_b, w_ref[wa:, :], preferred_element_type=F32)


def _out_ab(oa, ob, y, gcol, x2d, gsub, w_bf, n_heads, scale, tm):
    rows, d = x2d.shape
    wa, wb = oa.shape[1], ob.shape[1]
    return pl.pallas_call(
        functools.partial(_out_ab_kernel, n_heads=n_heads, scale=scale),
        grid=(rows // tm,),
        in_specs=[
            pl.BlockSpec((tm, wa), lambda i: (i, 0)),
            pl.BlockSpec((tm, wb), lambda i: (i, 0)),
            pl.BlockSpec((tm, wa + wb), lambda i: (i, gcol)),
            pl.BlockSpec((tm, d), lambda i: (i, 0)),
            pl.BlockSpec((1, LANES), lambda i: (0, 0)),
            pl.BlockSpec((wa + wb, d), lambda i: (0, 0)),
        ],
        out_specs=pl.BlockSpec((tm, d), lambda i: (i, 0)),
        out_shape=jax.ShapeDtypeStruct((rows, d), F32),
        compiler_params=pltpu.CompilerParams(dimension_semantics=("arbitrary",), vmem_limit_bytes=VMEM_LIMIT),
        name="out_ab",
    )(oa, ob, y, x2d, gsub, w_bf)


def _out_c_kernel(o_ref_in, g_ref, x_ref, w_ref, o_ref):
    mixed = (o_ref_in[...] * _silu(g_ref[...])).astype(BF16)
    o_ref[...] = x_ref[...] + jnp.dot(mixed, w_ref[...], preferred_element_type=F32)


def _out_c(o, y, gcol, x2d, w_bf, tm):
    rows, d = x2d.shape
    wc = o.shape[1]
    return pl.pallas_call(
        _out_c_kernel,
        grid=(rows // tm,),
        in_specs=[
            pl.BlockSpec((tm, wc), lambda i: (i, 0)),
            pl.BlockSpec((tm, wc), lambda i: (i, gcol)),
            pl.BlockSpec((tm, d), lambda i: (i, 0)),
            pl.BlockSpec((wc, d), lambda i: (0, 0)),
        ],
        out_specs=pl.BlockSpec((tm, d), lambda i: (i, 0)),
        out_shape=jax.ShapeDtypeStruct((rows, d), F32),
        compiler_params=pltpu.CompilerParams(dimension_semantics=("arbitrary",), vmem_limit_bytes=VMEM_LIMIT),
        name="out_c",
    )(o, y, x2d, w_bf)


def _rope_tables(pos):
    half = HEAD_DIM // 2
    inv_freq = ROPE_THETA ** (-jnp.arange(half, dtype=F32) / half)
    ang = pos.astype(F32)[:, None] * inv_freq[None, :]
    cos, sin = jnp.cos(ang), jnp.sin(ang)
    return jnp.tile(cos, (1, 4)), jnp.concatenate([-sin, sin, -sin, sin], axis=1)


def _to_qT(q, geom):
    qT = jnp.swapaxes(q.astype(BF16), 1, 2)
    return jnp.pad(qT, ((0, 0), (0, 0), (0, geom.lq - qT.shape[2])))


def _to_k(k, geom):
    k = k.astype(BF16)
    return jnp.pad(k, ((0, 0), (0, geom.lk - k.shape[1]), (0, 0)))


def _to_vT(v, geom):
    v = _to_k(v, geom)
    nb, _, w = v.shape
    return jnp.swapaxes(v.reshape(nb, geom.nk, geom.tk, w), 2, 3)


def _forward(cfg, x_prompt, x_sample, cache_a_k, cache_a_v, cache_b_k, cache_b_v, cache_c_k, cache_c_v,
             cache_c_kidx, meta_tokens, g_norm_ab, w_in_ab, g_qk_a, lam_a, g_sub_a, w_out_ab,
             g_norm_c, w_in_c, g_qk_c, w_out_c):
    d = cfg.d_model
    gp, gs = Geom(cfg, "prompt"), Geom(cfg, "sample")
    t_real = gp.t_real
    aw = cfg.a_heads * 2 * HEAD_DIM
    bw = cfg.b_heads * HEAD_DIM
    cw = cfg.c_heads * HEAD_DIM
    iw = cfg.idx_heads * HEAD_DIM
    n_sel_p = min(cfg.topk_max, cfg.seq // 4)
    n_sel_s = min(cfg.topk_max, (cfg.past_len + cfg.dec_seq) // 4)

    meta = jnp.broadcast_to(meta_tokens.astype(x_prompt.dtype)[None], (cfg.batch, cfg.n_meta, d))
    hp = jnp.concatenate([meta, x_prompt, jnp.zeros((cfg.batch, gp.lq - t_real, d), x_prompt.dtype)], axis=1)
    hp = hp.reshape(cfg.batch * gp.lq, d)
    rows_s = cfg.dec_batch * cfg.dec_seq
    tm_s = _round_up(rows_s, 8)
    hs = jnp.pad(x_sample.reshape(rows_s, d), ((0, tm_s - rows_s), (0, 0)))

    cos_p, sin_p = _rope_tables(jnp.arange(gp.lq, dtype=I32))
    cos_s, sin_s = _rope_tables(cfg.past_len + jnp.arange(tm_s, dtype=I32) % cfg.dec_seq)

    s_qk = HEAD_DIM ** -0.5
    s_q2 = s_qk * LOG2E
    groups_ab = ((0, aw, "norm_rope", 0, s_q2), (aw, aw + bw, "plain", 0, s_qk),
                 (aw + bw, 2 * aw + bw, "norm_rope", 1, 1.0), (2 * aw + bw, 2 * (aw + bw), "plain", 0, 1.0),
                 (2 * (aw + bw), 3 * (aw + bw), "plain", 0, 1.0), (3 * (aw + bw), 4 * (aw + bw), "plain", 0, 1.0))
    groups_c = tuple([(j * 512, (j + 1) * 512, "norm_rope", 0, s_q2) for j in range(cw // 512)]
                     + [(cw + j * 512, cw + (j + 1) * 512, "norm_rope", 1, 1.0) for j in range(cw // 512)]
                     + [(2 * cw, 3 * cw, "plain", 0, 1.0), (3 * cw, 4 * cw, "plain", 0, 1.0),
                        (4 * cw, 4 * cw + iw, "rope", 0, s_qk),
                        (4 * cw + iw, 4 * cw + iw + LANES, "kiwi", 0, cfg.idx_heads ** -0.5)])
    nc_pad = 4 * cw + iw + LANES

    def split_rows(y, geom, n_rows):
        if geom.kind == "prompt":
            return y.reshape(geom.nb, geom.lq, -1)[:, :n_rows]
        return y[:rows_s].reshape(geom.nb, cfg.dec_seq, -1)

    rows_ab_p, rows_ab_s, rows_c_p, rows_c_s = [], [], [], []
    for l in range(cfg.depth):
        i = l // 2
        if l % 2 == 0:
            lam_init = 0.8 - 0.6 * float(np.exp(-0.3 * l))
            w = w_in_ab[i]
            wq = [w[:, j * aw:(j + 1) * aw] for j in range(4)] + [w[:, 4 * aw + j * bw:4 * aw + (j + 1) * bw] for j in range(4)]
            w_perm = jnp.concatenate([wq[0], wq[4], wq[1], wq[5], wq[2], wq[6], wq[3], wq[7]], axis=1).astype(BF16)
            gq = jnp.tile(g_qk_a[i], (1, 2))
            gq = jnp.pad(gq, ((0, 8 - gq.shape[0]), (0, 0)))
            w_out = w_out_ab[i].astype(BF16)
            gsub = g_sub_a[i][None, :]
            gn = g_norm_ab[i][None, :]
            qk_w, v_off, g_off = aw + bw, 2 * (aw + bw), 3 * (aw + bw)
            for geom, which in ((gp, "p"), (gs, "s")):
                x2d, cos_t, sin_t, tm = (hp, cos_p, sin_p, TILE) if which == "p" else (hs, cos_s, sin_s, tm_s)
                y = _proj(x2d, gn, w_perm, cos_t, sin_t, gq, groups_ab, tm)
                if which == "p":
                    y3 = split_rows(y, geom, geom.lq)
                    new = y3[:, :t_real]
                    qT = _to_qT(y3[..., 0:qk_w], geom)
                    kbf = _to_k(y3[..., qk_w:2 * qk_w], geom)
                    vT = _to_vT(y3[..., v_off:v_off + qk_w], geom)
                else:
                    new = split_rows(y, geom, cfg.dec_seq)
                    qT = _to_qT(new[..., 0:qk_w], geom)
                    past_k = jnp.concatenate([cache_a_k[i].reshape(geom.nb, cfg.past_len, aw),
                                              cache_b_k[i].reshape(geom.nb, cfg.past_len, bw)], axis=-1)
                    past_v = jnp.concatenate([cache_a_v[i].reshape(geom.nb, cfg.past_len, aw),
                                              cache_b_v[i].reshape(geom.nb, cfg.past_len, bw)], axis=-1)
                    kbf = _to_k(jnp.concatenate([past_k, new[..., qk_w:2 * qk_w]], axis=1), geom)
                    vT = _to_vT(jnp.concatenate([past_v, new[..., v_off:v_off + qk_w]], axis=1), geom)
                n_rows = new.shape[1]
                rows = (new[..., qk_w:qk_w + aw].reshape(geom.nb, n_rows, cfg.a_heads, 2 * HEAD_DIM),
                        new[..., v_off:v_off + aw].reshape(geom.nb, n_rows, cfg.a_heads, 2 * HEAD_DIM),
                        new[..., qk_w + aw:2 * qk_w].reshape(geom.nb, n_rows, cfg.b_heads, HEAD_DIM),
                        new[..., v_off + aw:v_off + qk_w].reshape(geom.nb, n_rows, cfg.b_heads, HEAD_DIM))
                (rows_ab_p if which == "p" else rows_ab_s).append(rows)
                oa = _attn_a(geom, lam_a[i], qT, kbf, vT, cfg.a_heads, lam_init)
                ob = _attn_b(geom, qT, kbf, vT, cfg.b_heads // 2, cfg.a_heads)
                if which == "p":
                    hp = _out_ab(oa.reshape(-1, aw), ob.reshape(-1, bw), y, g_off // (aw + bw), hp, gsub, w_out,
                                 cfg.a_heads, 1.0 - lam_init, TILE)
                else:
                    oa_s = jnp.pad(oa[:, :cfg.dec_seq].reshape(rows_s, aw), ((0, tm_s - rows_s), (0, 0)))
                    ob_s = jnp.pad(ob[:, :cfg.dec_seq].reshape(rows_s, bw), ((0, tm_s - rows_s), (0, 0)))
                    hs = _out_ab(oa_s, ob_s, y, g_off // (aw + bw), hs, gsub, w_out, cfg.a_heads, 1.0 - lam_init, tm_s)
        else:
            w = w_in_c[i]
            w_pad = jnp.pad(w, ((0, 0), (0, nc_pad - w.shape[1]))).astype(BF16)
            gq = jnp.tile(g_qk_c[i], (1, 2))
            gq = jnp.pad(gq, ((0, 8 - gq.shape[0]), (0, 0)))
            w_out = w_out_c[i].astype(BF16)
            gn = g_norm_c[i][None, :]
            for geom, which in ((gp, "p"), (gs, "s")):
                x2d, cos_t, sin_t, tm = (hp, cos_p, sin_p, TILE) if which == "p" else (hs, cos_s, sin_s, tm_s)
                y = _proj(x2d, gn, w_pad, cos_t, sin_t, gq, groups_c, tm)
                if which == "p":
                    y3 = split_rows(y, geom, geom.lq)
                    new = y3[:, :t_real]
                    n_sel = n_sel_p
                    k_all, v_all = y3[..., cw:2 * cw], y3[..., 2 * cw:3 * cw]
                    ki_all = y3[..., 4 * cw + iw:4 * cw + iw + LANES]
                    q_new, qi_new, wi_new = y3[..., 0:cw], y3[..., 4 * cw:4 * cw + iw], ki_all[..., HEAD_DIM:HEAD_DIM + 8]
                else:
                    new = split_rows(y, geom, cfg.dec_seq)
                    n_sel = n_sel_s
                    k_all = jnp.concatenate([cache_c_k[i].reshape(geom.nb, cfg.past_len, cw), new[..., cw:2 * cw]], axis=1)
                    v_all = jnp.concatenate([cache_c_v[i].reshape(geom.nb, cfg.past_len, cw), new[..., 2 * cw:3 * cw]], axis=1)
                    ki_new = new[..., 4 * cw + iw:4 * cw + iw + LANES]
                    ki_past = jnp.pad(cache_c_kidx[i], ((0, 0), (0, 0), (0, LANES - HEAD_DIM)))
                    ki_all = jnp.concatenate([ki_past, jnp.where(jnp.arange(LANES) < HEAD_DIM, ki_new, 0.0)], axis=1)
                    q_new, qi_new, wi_new = new[..., 0:cw], new[..., 4 * cw:4 * cw + iw], ki_new[..., HEAD_DIM:HEAD_DIM + 8]
                n_rows = new.shape[1]
                rows = (new[..., cw:2 * cw].reshape(geom.nb, n_rows, cfg.c_heads, HEAD_DIM),
                        new[..., 2 * cw:3 * cw].reshape(geom.nb, n_rows, cfg.c_heads, HEAD_DIM),
                        new[..., 4 * cw + iw:4 * cw + iw + HEAD_DIM])
                (rows_c_p if which == "p" else rows_c_s).append(rows)
                qT = _to_qT(q_new, geom)
                kbf = _to_k(k_all, geom)
                vT = _to_vT(v_all, geom)
                qiT = _to_qT(qi_new, geom)
                wiT = jnp.pad(jnp.swapaxes(wi_new, 1, 2), ((0, 0), (0, 0), (0, geom.lq - wi_new.shape[1])))
                ki_bf = _to_k(jnp.where(jnp.arange(LANES) < HEAD_DIM, ki_all, 0.0), geom)
                mask = _select(geom, qiT, wiT, ki_bf, n_sel, cfg.idx_heads)
                oc = _attn_c(geom, qT, kbf, vT, mask, cfg.c_heads // 2)
                if which == "p":
                    hp = _out_c(oc.reshape(-1, cw), y, 3, hp, w_out, TILE)
                else:
                    oc_s = jnp.pad(oc[:, :cfg.dec_seq].reshape(rows_s, cw), ((0, tm_s - rows_s), (0, 0)))
                    hs = _out_c(oc_s, y, 3, hs, w_out, tm_s)

    y_prompt = hp.reshape(cfg.batch, gp.lq, d)[:, cfg.n_meta:t_real]
    y_sample = hs[:rows_s].reshape(cfg.dec_batch, cfg.dec_seq, d)

    def stack(rows, j):
        return jnp.stack([r[j] for r in rows], axis=0)

    return (y_prompt, y_sample,
            stack(rows_ab_p, 0), stack(rows_ab_p, 1), stack(rows_ab_p, 2), stack(rows_ab_p, 3),
            stack(rows_c_p, 0), stack(rows_c_p, 1), stack(rows_c_p, 2),
            stack(rows_ab_s, 0), stack(rows_ab_s, 1), stack(rows_ab_s, 2), stack(rows_ab_s, 3),
            stack(rows_c_s, 0), stack(rows_c_s, 1), stack(rows_c_s, 2))


def kernel(x_prompt, x_sample, cache_a_k, cache_a_v, cache_b_k, cache_b_v, cache_c_k, cache_c_v, cache_c_kidx,
           meta_tokens, g_norm_ab, w_in_ab, g_qk_a, lam_a, g_sub_a, w_out_ab, g_norm_c, w_in_c, g_qk_c, w_out_c):
    return _forward(Cfg(), x_prompt, x_sample, cache_a_k, cache_a_v, cache_b_k, cache_b_v, cache_c_k, cache_c_v,
                    cache_c_kidx, meta_tokens, g_norm_ab, w_in_ab, g_qk_a, lam_a, g_sub_a, w_out_ab,
                    g_norm_c, w_in_c, g_qk_c, w_out_c)
```

```python
import functools
from typing import NamedTuple

import numpy as np
import jax
import jax.numpy as jnp
from jax import lax
from jax.experimental import pallas as pl
from jax.experimental.pallas import tpu as pltpu

F32 = jnp.float32
BF16 = jnp.bfloat16
I32 = jnp.int32
I16 = jnp.int16

LANES = 128
TILE = 256
VMEM_LIMIT = 56 * 1024 * 1024

ROPE_THETA = 10000.0
EPS = 1e-6
NEG = -1e30
PAD_CHUNK = 2 ** 30
HEAD_DIM = 64
MIN16 = -2 ** 15
LOG2E = 1.4426950408889634
ONES_ROWS = 16
EXP_ZERO = -104.0


class Cfg(NamedTuple):
    d_model: int = 1024
    batch: int = 2
    seq: int = 8192
    depth: int = 4
    dec_batch: int = 8
    dec_seq: int = 16
    past_len: int = 2048
    chunk: int = 64
    n_meta: int = 16
    topk_max: int = 256
    a_heads: int = 4
    b_heads: int = 8
    c_heads: int = 16
    idx_heads: int = 4


def _round_up(x, m):
    return -(-x // m) * m


def _log2(n):
    l = int(n).bit_length() - 1
    assert (1 << l) == n, n
    return l


class Geom:
    def __init__(self, cfg, kind):
        self.kind = kind
        self.cfg = cfg
        self.sh = _log2(cfg.chunk)
        if kind == "prompt":
            self.nb = cfg.batch
            self.t_real = cfg.n_meta + cfg.seq
            self.lq = _round_up(self.t_real, 2 * TILE)
            self.lk = self.lq
            self.tq = TILE
        else:
            self.nb = cfg.dec_batch
            self.n_keys = cfg.past_len + cfg.dec_seq
            self.lq = LANES
            self.lk = _round_up(self.n_keys, 2 * TILE)
            self.tq = LANES
        self.tk = TILE
        self.nq = self.lq // self.tq
        self.nk = self.lk // self.tk

    def qpos(self, i):
        return i if self.kind == "prompt" else i + self.cfg.past_len

    def _cid_prompt(self, p):
        c = self.cfg
        body = lax.shift_right_logical(jnp.maximum(p - c.n_meta, 0), self.sh) + 1
        return jnp.where(p >= self.t_real, PAD_CHUNK, jnp.where(p < c.n_meta, 0, body))

    def qcid(self, pos):
        if self.kind == "prompt":
            return self._cid_prompt(pos)
        return lax.shift_right_logical(pos, self.sh)

    def kcid(self, j):
        if self.kind == "prompt":
            return self._cid_prompt(j)
        return jnp.where(j >= self.n_keys, PAD_CHUNK, lax.shift_right_logical(j, self.sh))

    def kend_cid(self, qpos):
        c = self.cfg
        if self.kind == "prompt":
            end = jnp.minimum(c.n_meta + c.chunk * self._cid_prompt(jnp.minimum(qpos, self.t_real - 1)),
                              self.t_real)
            return jnp.where(qpos >= self.t_real, self.lk, end)
        return jnp.minimum(c.chunk * (lax.shift_right_logical(qpos, self.sh) + 1), self.n_keys)

    def kend_pos(self, qpos):
        return jnp.minimum(qpos, self.lk)

    def tiles(self, q0, rule):
        kend = self.kend_cid if rule == "cid" else self.kend_pos
        first = kend(self.qpos(q0))
        last = kend(self.qpos(q0 + self.tq - 1))
        n_vis = (last + self.tk - 1) // self.tk
        n_full = first // self.tk
        return n_vis, n_full


def _proj_kernel(x_ref, g_ref, w_ref, cos_ref, sin_ref, gq_ref, *o_refs, groups):
    x = x_ref[...]
    ms = jnp.mean(x * x, axis=-1, keepdims=True)
    h = (x * lax.rsqrt(ms + EPS) * g_ref[...]).astype(BF16)
    cos = cos_ref[...]
    sin_s = sin_ref[...]
    lane = lax.broadcasted_iota(I32, (1, LANES), 1)
    first_half = (lane & (HEAD_DIM - 1)) < (HEAD_DIM // 2)
    r = lax.broadcasted_iota(I32, (LANES, LANES), 0)
    c = lax.broadcasted_iota(I32, (LANES, LANES), 1)
    bd = jnp.where(lax.shift_right_logical(r, 6) == lax.shift_right_logical(c, 6), 1.0, 0.0).astype(BF16)

    def rope(y):
        part = jnp.where(first_half, pltpu.roll(y, LANES - HEAD_DIM // 2, 1), pltpu.roll(y, HEAD_DIM // 2, 1))
        return y * cos + part * sin_s

    def emit(out, j, sinks):
        for (oi, mode, off) in sinks:
            o_ref = o_refs[oi]
            lo_, hi_ = off + j * LANES, off + (j + 1) * LANES
            if mode == "row":
                o_ref[:, lo_:hi_] = out.astype(o_ref.dtype)
            elif mode == "T":
                o_ref[lo_:hi_, :] = out.T.astype(o_ref.dtype)
            elif mode == "row_lo":
                o_ref[:, lo_:hi_] = jnp.where(lane < HEAD_DIM, out, 0.0).astype(o_ref.dtype)
            else:
                o_ref[...] = out.T[HEAD_DIM:HEAD_DIM + 8, :].astype(o_ref.dtype)

    for (c0, c1, kind, gi, scale, sinks) in groups:
        y = jnp.dot(h, w_ref[:, c0:c1], preferred_element_type=F32)
        for j in range((c1 - c0) // LANES):
            yb = y[:, j * LANES:(j + 1) * LANES]
            if kind == "plain":
                out = yb
            elif kind == "norm_rope":
                sq = yb * yb
                hi = sq.astype(BF16)
                lo = (sq - hi.astype(F32)).astype(BF16)
                ss = jnp.dot(hi, bd, preferred_element_type=F32) + jnp.dot(lo, bd, preferred_element_type=F32)
                yb = yb * lax.rsqrt(ss * (1.0 / HEAD_DIM) + EPS) * gq_ref[gi:gi + 1, :]
                out = rope(yb)
            elif kind == "rope":
                out = rope(yb)
            else:
                out = jnp.where(lane < HEAD_DIM, rope(yb), jnp.where(lane < HEAD_DIM + 4, yb, 0.0))
            if scale != 1.0:
                out = out * jnp.where(lane < HEAD_DIM, 1.0, scale) if kind == "kiwi" else out * scale
            emit(out, j, sinks)


def _proj(x2d, g, w_bf, cos_t, sin_t, gq, groups, tm, outs):
    rows, d = x2d.shape
    n = w_bf.shape[1]
    period = cos_t.shape[0] // tm
    res = pl.pallas_call(
        functools.partial(_proj_kernel, groups=groups),
        grid=(rows // tm,),
        in_specs=[
            pl.BlockSpec((tm, d), lambda i: (i, 0)),
            pl.BlockSpec((1, d), lambda i: (0, 0)),
            pl.BlockSpec((d, n), lambda i: (0, 0)),
            pl.BlockSpec((tm, LANES), lambda i: (i % period, 0)),
            pl.BlockSpec((tm, LANES), lambda i: (i % period, 0)),
            pl.BlockSpec(gq.shape, lambda i: (0, 0)),
        ],
        out_specs=[pl.BlockSpec(blk, imap) for (_, _, blk, imap) in outs],
        out_shape=[jax.ShapeDtypeStruct(shape, dt) for (shape, dt, _, _) in outs],
        compiler_params=pltpu.CompilerParams(dimension_semantics=("arbitrary",), vmem_limit_bytes=VMEM_LIMIT),
        name="proj",
    )(x2d, g, w_bf, cos_t, sin_t, gq)
    return res


def _row_sinks(groups):
    return tuple((c0, c1, kind, gi, scale, ((0, "row", c0),)) for (c0, c1, kind, gi, scale) in groups)


def _half_rows(qT):
    qf = qT.astype(F32)
    row = lax.broadcasted_iota(I32, qf.shape, 0)
    return (jnp.where(row < HEAD_DIM, qf, 0.0).astype(BF16), jnp.where(row >= HEAD_DIM, qf, 0.0).astype(BF16))


def _pipelined_softmax(nk, n_full, n_vis, qk, softmax, pv):
    def step(kt, slot, masked):
        qk(jnp.minimum(kt + 1, nk - 1), 1 - slot)
        pv(jnp.maximum(kt - 1, 0), 1 - slot)
        softmax(kt, slot, masked)

    def pair(masked):
        def body(j, carry):
            step(2 * j, 0, masked)
            step(2 * j + 1, 1, masked)
            return carry
        return body

    n_pairs = (n_vis + 1) // 2
    n_full_pairs = n_full // 2
    qk(0, 0)
    lax.fori_loop(0, n_full_pairs, pair(False), 0)
    lax.fori_loop(n_full_pairs, n_pairs, pair(True), 0)
    pv(2 * n_pairs - 1, 1)


def _softmax_tile(s, m_ref, al_ref, p_ref, slot, idx):
    m_old = m_ref[idx]
    m_new = jnp.maximum(m_old, jnp.max(s, axis=0, keepdims=True))
    al_ref[idx] = jnp.exp2(m_old - m_new)
    p_ref[slot, idx] = jnp.exp2(s - m_new).astype(BF16)
    m_ref[idx] = m_new


def _init_pipeline(m_s, acc_s, p_buf, al_s):
    m_s[...] = jnp.full(m_s.shape, NEG, F32)
    acc_s[...] = jnp.zeros(acc_s.shape, F32)
    p_buf[1] = jnp.zeros(p_buf.shape[1:], BF16)
    al_s[...] = jnp.ones(al_s.shape, F32)


def _tile_rows(kt, tk):
    return pl.ds(pl.multiple_of(kt * tk, tk), tk)


def _attn_a_kernel(lam_ref, q_ref, k_ref, v_ref, o_ref, m_s, acc_s, s_buf, p_buf, al_s, *, geom, lam_init):
    tq, tk = geom.tq, geom.tk
    dv = 2 * HEAD_DIM
    q0 = pl.program_id(2) * tq
    qm = _half_rows(q_ref[...])
    qcid = geom.qcid(geom.qpos(q0 + lax.broadcasted_iota(I32, (1, tq), 1)))
    n_vis, n_full = geom.tiles(q0, "cid")
    _init_pipeline(m_s, acc_s, p_buf, al_s)
    ones = jnp.ones((ONES_ROWS, tk), BF16)

    def qk(kt, slot):
        k = k_ref[_tile_rows(kt, tk), :]
        for mp in range(2):
            s_buf[slot, mp] = jnp.dot(k, qm[mp], preferred_element_type=F32)

    def pv(kt, slot):
        va = jnp.concatenate([v_ref[kt], ones], axis=0)
        for mp in range(2):
            acc_s[mp] = al_s[mp] * acc_s[mp] + jnp.dot(va, p_buf[slot, mp], preferred_element_type=F32)

    def softmax(kt, slot, masked):
        if masked:
            vis = geom.kcid(kt * tk + lax.broadcasted_iota(I32, (tk, 1), 0)) <= qcid
        for mp in range(2):
            s = s_buf[slot, mp]
            if masked:
                s = jnp.where(vis, s, NEG)
            _softmax_tile(s, m_s, al_s, p_buf, slot, mp)

    _pipelined_softmax(geom.nk, n_full, n_vis, qk, softmax, pv)
    lp = lam_ref[...]
    lam = (jnp.exp(jnp.sum(lp[0:1] * lp[1:2], axis=-1, keepdims=True))
           - jnp.exp(jnp.sum(lp[2:3] * lp[3:4], axis=-1, keepdims=True)) + lam_init)
    oT = acc_s[0, :dv] / acc_s[0, dv:dv + 1] - lam * (acc_s[1, :dv] / acc_s[1, dv:dv + 1])
    o_ref[...] = oT.T


def _attn_a(geom, lam_p, qT, kbf, vT, n_heads, lam_init):
    nb, tq, tk, nk = geom.nb, geom.tq, geom.tk, geom.nk
    return pl.pallas_call(
        functools.partial(_attn_a_kernel, geom=geom, lam_init=lam_init),
        grid=(nb, n_heads, geom.nq),
        in_specs=[
            pl.BlockSpec(lam_p.shape, lambda b, h, i: (0, 0)),
            pl.BlockSpec((None, LANES, tq), lambda b, h, i: (b, h, i)),
            pl.BlockSpec((None, geom.lk, LANES), lambda b, h, i: (b, 0, h)),
            pl.BlockSpec((None, nk, LANES, tk), lambda b, h, i: (b, 0, h, 0)),
        ],
        out_specs=pl.BlockSpec((None, tq, LANES), lambda b, h, i: (b, i, h)),
        out_shape=jax.ShapeDtypeStruct((nb, geom.lq, n_heads * LANES), F32),
        scratch_shapes=[pltpu.VMEM((2, 1, tq), F32), pltpu.VMEM((2, LANES + ONES_ROWS, tq), F32),
                        pltpu.VMEM((2, 2, tk, tq), F32), pltpu.VMEM((2, 2, tk, tq), BF16), pltpu.VMEM((2, 1, tq), F32)],
        compiler_params=pltpu.CompilerParams(dimension_semantics=("arbitrary",) * 3, vmem_limit_bytes=VMEM_LIMIT),
        name="attn_a",
    )(lam_p, qT, kbf, vT)


def _attn_b_kernel(q_ref, k_ref, v_ref, o_ref, c_s, acc_s, *, geom):
    tq, tk = geom.tq, geom.tk
    q0 = pl.program_id(2) * tq
    qh = _half_rows(q_ref[...])
    qpos = geom.qpos(q0 + lax.broadcasted_iota(I32, (1, tq), 1))
    n_vis, n_full = geom.tiles(q0, "pos")
    c_s[...] = jnp.zeros(c_s.shape, F32)
    acc_s[...] = jnp.zeros(acc_s.shape, F32)
    r = lax.broadcasted_iota(I32, (tk, tk), 0)
    c = lax.broadcasted_iota(I32, (tk, tk), 1)
    upper = jnp.where(c > r, 1.0, 0.0).astype(BF16)

    def body(kt, masked):
        k = k_ref[_tile_rows(kt, tk), :]
        v = v_ref[kt]
        if masked:
            vis = (kt * tk + lax.broadcasted_iota(I32, (tk, 1), 0)) < qpos
        for hh in range(2):
            z = jnp.dot(k, qh[hh], preferred_element_type=F32)
            sp = jnp.maximum(z, 0.0) + jnp.log1p(jnp.exp(-jnp.abs(z)))
            log_keep = -sp
            log_beta = z - sp
            if masked:
                log_keep = jnp.where(vis, log_keep, 0.0)
            hi = log_keep.astype(BF16)
            lo = (log_keep - hi.astype(F32)).astype(BF16)
            after = (jnp.dot(upper, hi, preferred_element_type=F32)
                     + jnp.dot(upper, lo, preferred_element_type=F32) + c_s[hh])
            a = jnp.exp(log_beta + after)
            if masked:
                a = jnp.where(vis, a, 0.0)
            vh = v[hh * HEAD_DIM:(hh + 1) * HEAD_DIM, :]
            acc_s[hh] = acc_s[hh] + jnp.dot(vh, a.astype(BF16), preferred_element_type=F32)
            c_s[hh] = c_s[hh] + jnp.sum(log_keep, axis=0, keepdims=True)

    def alive():
        return jnp.max(c_s[...]) > EXP_ZERO

    lax.fori_loop(0, n_vis - n_full, lambda i, carry: (body(n_vis - 1 - i, True), carry)[1], 0)

    def full_tile(state):
        i, _ = state
        body(n_full - 1 - i, False)
        return i + 1, alive()

    lax.while_loop(lambda state: (state[0] < n_full) & state[1], full_tile, (jnp.int32(0), alive()))
    o_ref[...] = jnp.concatenate([acc_s[0], acc_s[1]], axis=0).T


def _attn_b(geom, qT, kbf, vT, n_pairs, blk0):
    nb, tq, tk, nk = geom.nb, geom.tq, geom.tk, geom.nk
    return pl.pallas_call(
        functools.partial(_attn_b_kernel, geom=geom),
        grid=(nb, n_pairs, geom.nq),
        in_specs=[
            pl.BlockSpec((None, LANES, tq), lambda b, h, i: (b, blk0 + h, i)),
            pl.BlockSpec((None, geom.lk, LANES), lambda b, h, i: (b, 0, blk0 + h)),
            pl.BlockSpec((None, nk, LANES, tk), lambda b, h, i: (b, 0, blk0 + h, 0)),
        ],
        out_specs=pl.BlockSpec((None, tq, LANES), lambda b, h, i: (b, i, h)),
        out_shape=jax.ShapeDtypeStruct((nb, geom.lq, n_pairs * LANES), F32),
        scratch_shapes=[pltpu.VMEM((2, 1, tq), F32), pltpu.VMEM((2, HEAD_DIM, tq), F32)],
        compiler_params=pltpu.CompilerParams(dimension_semantics=("arbitrary",) * 3, vmem_limit_bytes=VMEM_LIMIT),
        name="attn_b",
    )(qT, kbf, vT)


def _select_bias(qi_ref, wi_ref, ki_ref, hi_s, lo_s, bias_s, *, geom, n_sel, n_idx, q0, n_vis, n_fill):
    tq, tk = geom.tq, geom.tk
    qcid = geom.qcid(geom.qpos(q0 + lax.broadcasted_iota(I32, (1, tq), 1)))
    zpad = jnp.zeros((LANES - HEAD_DIM, tq), BF16)
    qh = [jnp.concatenate([qi_ref[h * HEAD_DIM:(h + 1) * HEAD_DIM, :], zpad], axis=0) for h in range(n_idx)]
    wi = wi_ref[...]

    def vis_of(kt):
        return geom.kcid(kt * tk + lax.broadcasted_iota(I32, (tk, 1), 0)) <= qcid

    def rows(kt):
        return pl.ds(pl.multiple_of(kt * tk, tk), tk)

    def score_tile(kt, carry):
        ki = ki_ref[rows(kt), :]
        sc = jnp.zeros((tk, tq), F32)
        for h in range(n_idx):
            rel = jnp.maximum(jnp.dot(ki, qh[h], preferred_element_type=F32), 0.0)
            sc = sc + wi[h:h + 1, :] * rel
        sc = jnp.where(sc == 0.0, 0.0, sc)
        sc = jnp.where(vis_of(kt), sc, NEG)
        bits = lax.bitcast_convert_type(sc, I32)
        key = jnp.where(bits < 0, bits ^ jnp.int32(0x7FFFFFFF), bits)
        hi_s[rows(kt), :] = lax.shift_right_arithmetic(key, 16).astype(I16)
        lo_s[rows(kt), :] = ((key & 0xFFFF) + MIN16).astype(I16)
        return carry

    lax.fori_loop(0, n_vis, score_tile, 0)

    def count16(ref, pred_fn):
        def step(kt, acc):
            hit = jnp.where(pred_fn(ref[rows(kt), :]), jnp.int16(1), jnp.int16(0))
            for j in range(tk // 16):
                acc = acc + hit[j * 16:(j + 1) * 16, :]
            return acc
        acc = lax.fori_loop(0, n_vis, step, jnp.zeros((16, tq), I16))
        return jnp.sum(acc.astype(I32), axis=0, keepdims=True)

    def search16(ref, target):
        def bit_step(i, prefix):
            cand = prefix | lax.shift_left(jnp.int32(1), 15 - i)
            cand_s = (cand + MIN16).astype(I16)
            cnt = count16(ref, lambda x: x >= cand_s)
            return jnp.where(cnt >= target, cand, prefix)
        return (lax.fori_loop(0, 16, bit_step, jnp.zeros((1, tq), I32)) + MIN16).astype(I16)

    hi_thr = search16(hi_s, n_sel)
    n_low = n_sel - count16(hi_s, lambda x: x > hi_thr)

    def bucket_tile(kt, carry):
        lo_s[rows(kt), :] = jnp.where(hi_s[rows(kt), :] == hi_thr, lo_s[rows(kt), :], jnp.int16(MIN16))
        return carry

    lax.fori_loop(0, n_vis, bucket_tile, 0)
    lo_thr = search16(lo_s, n_low)
    need = (n_low - count16(lo_s, lambda x: x > lo_thr)).astype(F32)

    r = lax.broadcasted_iota(I32, (tk, tk), 0)
    c = lax.broadcasted_iota(I32, (tk, tk), 1)
    lower = jnp.where(c < r, 1.0, 0.0).astype(BF16)

    def bias_tile(kt, run):
        hi = hi_s[rows(kt), :]
        lo = lo_s[rows(kt), :]
        same = hi == hi_thr
        gt = jnp.where((hi > hi_thr) | (same & (lo > lo_thr)), jnp.int16(1), jnp.int16(0)).astype(I32) != 0
        eq_i = jnp.where(same & (lo == lo_thr), jnp.int16(1), jnp.int16(0)).astype(I32)
        eq = eq_i.astype(F32)
        rank = jnp.dot(lower, eq.astype(BF16), preferred_element_type=F32) + run
        sel = (gt | ((eq_i != 0) & (rank < need))) & vis_of(kt)
        bias_s[rows(kt), :] = jnp.where(sel, 0.0, NEG)
        return run + jnp.sum(eq, axis=0, keepdims=True)

    lax.fori_loop(0, n_vis, bias_tile, jnp.zeros((1, tq), F32))

    def fill_tile(kt, carry):
        bias_s[rows(kt), :] = jnp.full((tk, tq), NEG, F32)
        return carry

    lax.fori_loop(n_vis, n_fill, fill_tile, 0)


def _dsa_kernel(qi_ref, wi_ref, ki_ref, q_ref, k_ref, v_ref, o_ref, hi_s, lo_s, bias_s, m_s, acc_s, s_buf, p_buf,
                al_s, *, geom, n_sel, n_idx):
    tq, tk = geom.tq, geom.tk
    q0 = pl.program_id(1) * tq
    n_vis, _ = geom.tiles(q0, "cid")
    n_pairs = (n_vis + 1) // 2

    @pl.when(pl.program_id(2) == 0)
    def _():
        _select_bias(qi_ref, wi_ref, ki_ref, hi_s, lo_s, bias_s, geom=geom, n_sel=n_sel, n_idx=n_idx, q0=q0,
                     n_vis=n_vis, n_fill=2 * n_pairs)

    qh = _half_rows(q_ref[...])
    _init_pipeline(m_s, acc_s, p_buf, al_s)
    ones = jnp.ones((ONES_ROWS, tk), BF16)

    def qk(kt, slot):
        k = k_ref[_tile_rows(kt, tk), :]
        for hh in range(2):
            s_buf[slot, hh] = jnp.dot(k, qh[hh], preferred_element_type=F32)

    def pv(kt, slot):
        v = v_ref[kt]
        for hh in range(2):
            va = jnp.concatenate([v[hh * HEAD_DIM:(hh + 1) * HEAD_DIM, :], ones], axis=0)
            acc_s[hh] = al_s[hh] * acc_s[hh] + jnp.dot(va, p_buf[slot, hh], preferred_element_type=F32)

    def softmax(kt, slot, masked):
        bias = bias_s[_tile_rows(kt, tk), :]
        for hh in range(2):
            _softmax_tile(s_buf[slot, hh] + bias, m_s, al_s, p_buf, slot, hh)

    _pipelined_softmax(geom.nk, 0, n_vis, qk, softmax, pv)
    o_ref[...] = jnp.concatenate([acc_s[0, :HEAD_DIM] / acc_s[0, HEAD_DIM:HEAD_DIM + 1],
                                  acc_s[1, :HEAD_DIM] / acc_s[1, HEAD_DIM:HEAD_DIM + 1]], axis=0).T


def _dsa(geom, qiT, wiT, ki_bf, qT, kbf, vT, n_sel, n_idx, n_pairs):
    nb, tq, tk, nk = geom.nb, geom.tq, geom.tk, geom.nk
    return pl.pallas_call(
        functools.partial(_dsa_kernel, geom=geom, n_sel=n_sel, n_idx=n_idx),
        grid=(nb, geom.nq, n_pairs),
        in_specs=[
            pl.BlockSpec((None, n_idx * HEAD_DIM, tq), lambda b, i, h: (b, 0, i)),
            pl.BlockSpec((None, 8, tq), lambda b, i, h: (b, 0, i)),
            pl.BlockSpec((None, geom.lk, LANES), lambda b, i, h: (b, 0, 0)),
            pl.BlockSpec((None, LANES, tq), lambda b, i, h: (b, h, i)),
            pl.BlockSpec((None, geom.lk, LANES), lambda b, i, h: (b, 0, h)),
            pl.BlockSpec((None, nk, LANES, tk), lambda b, i, h: (b, 0, h, 0)),
        ],
        out_specs=pl.BlockSpec((None, tq, LANES), lambda b, i, h: (b, i, h)),
        out_shape=jax.ShapeDtypeStruct((nb, geom.lq, n_pairs * LANES), F32),
        scratch_shapes=[pltpu.VMEM((geom.lk, tq), I16), pltpu.VMEM((geom.lk, tq), I16), pltpu.VMEM((geom.lk, tq), F32),
                        pltpu.VMEM((2, 1, tq), F32), pltpu.VMEM((2, HEAD_DIM + ONES_ROWS, tq), F32),
                        pltpu.VMEM((2, 2, tk, tq), F32), pltpu.VMEM((2, 2, tk, tq), BF16), pltpu.VMEM((2, 1, tq), F32)],
        compiler_params=pltpu.CompilerParams(dimension_semantics=("arbitrary",) * 3, vmem_limit_bytes=VMEM_LIMIT),
        name="dsa",
    )(qiT, wiT, ki_bf, qT, kbf, vT)


def _silu(g):
    return g * (1.0 / (1.0 + jnp.exp(-g)))


def _out_ab_kernel(oa_ref, ob_ref, g_ref, x_ref, gsub_ref, w_ref, o_ref, *, n_heads, scale):
    wa = n_heads * LANES
    acc = x_ref[...]
    g = g_ref[...]
    for h in range(n_heads):
        blk = oa_ref[:, h * LANES:(h + 1) * LANES]
        ms = jnp.mean(blk * blk, axis=-1, keepdims=True)
        nrm = (blk * lax.rsqrt(ms + EPS) * gsub_ref[...]) * scale
        mixed = (nrm * _silu(g[:, h * LANES:(h + 1) * LANES])).astype(BF16)
        acc = acc + jnp.dot(mixed, w_ref[h * LANES:(h + 1) * LANES, :], preferred_element_type=F32)
    mixed_b = (ob_ref[...] * _silu(g[:, wa:])).astype(BF16)
    o_ref[...] = acc + jnp.dot(mixed_b, w_ref[wa:, :], preferred_element_type=F32)


def _out_ab(oa, ob, y, gcol, x2d, gsub, w_bf, n_heads, scale, tm):
    rows, d = x2d.shape
    wa, wb = oa.shape[1], ob.shape[1]
    return pl.pallas_call(
        functools.partial(_out_ab_kernel, n_heads=n_heads, scale=scale),
        grid=(rows // tm,),
        in_specs=[
            pl.BlockSpec((tm, wa), lambda i: (i, 0)),
            pl.BlockSpec((tm, wb), lambda i: (i, 0)),
            pl.BlockSpec((tm, wa + wb), lambda i: (i, gcol)),
            pl.BlockSpec((tm, d), lambda i: (i, 0)),
            pl.BlockSpec((1, LANES), lambda i: (0, 0)),
            pl.BlockSpec((wa + wb, d), lambda i: (0, 0)),
        ],
        out_specs=pl.BlockSpec((tm, d), lambda i: (i, 0)),
        out_shape=jax.ShapeDtypeStruct((rows, d), F32),
        compiler_params=pltpu.CompilerParams(dimension_semantics=("arbitrary",), vmem_limit_bytes=VMEM_LIMIT),
        name="out_ab",
    )(oa, ob, y, x2d, gsub, w_bf)


def _out_c_kernel(o_ref_in, g_ref, x_ref, w_ref, o_ref):
    mixed = (o_ref_in[...] * _silu(g_ref[...])).astype(BF16)
    o_ref[...] = x_ref[...] + jnp.dot(mixed, w_ref[...], preferred_element_type=F32)


def _out_c(o, y, gcol, x2d, w_bf, tm):
    rows, d = x2d.shape
    wc = o.shape[1]
    return pl.pallas_call(
        _out_c_kernel,
        grid=(rows // tm,),
        in_specs=[
            pl.BlockSpec((tm, wc), lambda i: (i, 0)),
            pl.BlockSpec((tm, wc), lambda i: (i, gcol)),
            pl.BlockSpec((tm, d), lambda i: (i, 0)),
            pl.BlockSpec((wc, d), lambda i: (0, 0)),
        ],
        out_specs=pl.BlockSpec((tm, d), lambda i: (i, 0)),
        out_shape=jax.ShapeDtypeStruct((rows, d), F32),
        compiler_params=pltpu.CompilerParams(dimension_semantics=("arbitrary",), vmem_limit_bytes=VMEM_LIMIT),
        name="out_c",
    )(o, y, x2d, w_bf)


def _rope_tables(pos):
    half = HEAD_DIM // 2
    inv_freq = ROPE_THETA ** (-jnp.arange(half, dtype=F32) / half)
    ang = pos.astype(F32)[:, None] * inv_freq[None, :]
    cos, sin = jnp.cos(ang), jnp.sin(ang)
    return jnp.tile(cos, (1, 4)), jnp.concatenate([-sin, sin, -sin, sin], axis=1)


def _to_qT(q, geom):
    qT = jnp.swapaxes(q.astype(BF16), 1, 2)
    return jnp.pad(qT, ((0, 0), (0, 0), (0, geom.lq - qT.shape[2])))


def _to_k(k, geom):
    k = k.astype(BF16)
    return jnp.pad(k, ((0, 0), (0, geom.lk - k.shape[1]), (0, 0)))


def _to_vT(v, geom):
    v = _to_k(v, geom)
    nb, _, w = v.shape
    return jnp.swapaxes(v.reshape(nb, geom.nk, geom.tk, w), 2, 3)


def _forward(cfg, x_prompt, x_sample, cache_a_k, cache_a_v, cache_b_k, cache_b_v, cache_c_k, cache_c_v,
             cache_c_kidx, meta_tokens, g_norm_ab, w_in_ab, g_qk_a, lam_a, g_sub_a, w_out_ab,
             g_norm_c, w_in_c, g_qk_c, w_out_c):
    d = cfg.d_model
    gp, gs = Geom(cfg, "prompt"), Geom(cfg, "sample")
    t_real = gp.t_real
    nbp, lq, tm = gp.nb, gp.lq, TILE
    tpb = lq // tm
    aw = cfg.a_heads * 2 * HEAD_DIM
    bw = cfg.b_heads * HEAD_DIM
    cw = cfg.c_heads * HEAD_DIM
    iw = cfg.idx_heads * HEAD_DIM
    abw = aw + bw
    n_sel_p = min(cfg.topk_max, cfg.seq // 4)
    n_sel_s = min(cfg.topk_max, (cfg.past_len + cfg.dec_seq) // 4)

    meta = jnp.broadcast_to(meta_tokens.astype(x_prompt.dtype)[None], (cfg.batch, cfg.n_meta, d))
    hp = jnp.concatenate([meta, x_prompt, jnp.zeros((cfg.batch, lq - t_real, d), x_prompt.dtype)], axis=1)
    hp = hp.reshape(cfg.batch * lq, d)
    rows_s = cfg.dec_batch * cfg.dec_seq
    tm_s = _round_up(rows_s, 8)
    hs = jnp.pad(x_sample.reshape(rows_s, d), ((0, tm_s - rows_s), (0, 0)))

    cos_p, sin_p = _rope_tables(jnp.arange(lq, dtype=I32))
    cos_s, sin_s = _rope_tables(cfg.past_len + jnp.arange(tm_s, dtype=I32) % cfg.dec_seq)

    s_qk = HEAD_DIM ** -0.5
    s_q2 = s_qk * LOG2E
    groups_ab = ((0, aw, "norm_rope", 0, s_q2), (aw, abw, "plain", 0, s_qk),
                 (abw, abw + aw, "norm_rope", 1, 1.0), (abw + aw, 2 * abw, "plain", 0, 1.0),
                 (2 * abw, 3 * abw, "plain", 0, 1.0), (3 * abw, 4 * abw, "plain", 0, 1.0))
    sinks_ab = (((0, "T", 0),), ((0, "T", aw),), ((1, "row", 0), (3, "row", 0)), ((1, "row", aw), (3, "row", aw)),
                ((2, "T", 0), (3, "row", abw)), ((4, "row", 0),))
    half = cw // 2
    groups_c = tuple([(j * half, (j + 1) * half, "norm_rope", 0, s_q2) for j in range(2)]
                     + [(cw + j * half, cw + (j + 1) * half, "norm_rope", 1, 1.0) for j in range(2)]
                     + [(2 * cw, 3 * cw, "plain", 0, 1.0), (3 * cw, 4 * cw, "plain", 0, 1.0),
                        (4 * cw, 4 * cw + iw, "rope", 0, s_qk),
                        (4 * cw + iw, 4 * cw + iw + LANES, "kiwi", 0, cfg.idx_heads ** -0.5)])
    sinks_c = tuple([((0, "T", j * half),) for j in range(2)]
                    + [((1, "row", j * half), (3, "row", j * half)) for j in range(2)]
                    + [((2, "T", 0), (3, "row", cw)), ((4, "row", 0),), ((5, "T", 0),),
                       ((6, "row_lo", 0), (7, "T_wi", 0), (8, "row", 0))])
    nc_pad = 4 * cw + iw + LANES

    def with_sinks(groups, sinks):
        return tuple(g + (s,) for g, s in zip(groups, sinks))

    def prompt_outs(width, extra):
        outs = [((nbp, width, lq), BF16, (None, width, tm), lambda i: (i // tpb, 0, i % tpb)),
                ((nbp, lq, width), BF16, (None, tm, width), lambda i: (i // tpb, i % tpb, 0)),
                ((nbp, gp.nk, width, tm), BF16, (None, None, width, tm), lambda i: (i // tpb, i % tpb, 0, 0)),
                ((nbp, lq, 2 * width), F32, (None, tm, 2 * width), lambda i: (i // tpb, i % tpb, 0)),
                ((nbp * lq, width), F32, (tm, width), lambda i: (i, 0))]
        return outs + extra

    extra_c = [((nbp, iw, lq), BF16, (None, iw, tm), lambda i: (i // tpb, 0, i % tpb)),
               ((nbp, lq, LANES), BF16, (None, tm, LANES), lambda i: (i // tpb, i % tpb, 0)),
               ((nbp, 8, lq), F32, (None, 8, tm), lambda i: (i // tpb, 0, i % tpb)),
               ((nbp, lq, LANES), F32, (None, tm, LANES), lambda i: (i // tpb, i % tpb, 0))]

    def sample_proj(w_bf, gn, gq, groups):
        outs = [((tm_s, w_bf.shape[1]), F32, (tm_s, w_bf.shape[1]), lambda i: (i, 0))]
        y = _proj(hs, gn, w_bf, cos_s, sin_s, gq, _row_sinks(groups), tm_s, outs)[0]
        return y, y[:rows_s].reshape(gs.nb, cfg.dec_seq, -1)

    def sample_rows(o, width):
        return jnp.pad(o[:, :cfg.dec_seq].reshape(rows_s, width), ((0, tm_s - rows_s), (0, 0)))

    rows_ab_p, rows_ab_s, rows_c_p, rows_c_s = [], [], [], []
    for l in range(cfg.depth):
        i = l // 2
        if l % 2 == 0:
            lam_init = 0.8 - 0.6 * float(np.exp(-0.3 * l))
            w = w_in_ab[i]
            wq = [w[:, j * aw:(j + 1) * aw] for j in range(4)] + [w[:, 4 * aw + j * bw:4 * aw + (j + 1) * bw] for j in range(4)]
            w_perm = jnp.concatenate([wq[0], wq[4], wq[1], wq[5], wq[2], wq[6], wq[3], wq[7]], axis=1).astype(BF16)
            gq = jnp.tile(g_qk_a[i], (1, 2))
            gq = jnp.pad(gq, ((0, 8 - gq.shape[0]), (0, 0)))
            w_out = w_out_ab[i].astype(BF16)
            gsub = g_sub_a[i][None, :]
            gn = g_norm_ab[i][None, :]
            a_shape, b_shape = (cfg.a_heads, 2 * HEAD_DIM), (cfg.b_heads, HEAD_DIM)

            qT, kbf, vT, kv, gates = _proj(hp, gn, w_perm, cos_p, sin_p, gq, with_sinks(groups_ab, sinks_ab), tm,
                                           prompt_outs(abw, []))
            new = kv[:, :t_real]
            rows_ab_p.append((new[..., 0:aw].reshape(nbp, t_real, *a_shape),
                              new[..., abw:abw + aw].reshape(nbp, t_real, *a_shape),
                              new[..., aw:abw].reshape(nbp, t_real, *b_shape),
                              new[..., abw + aw:2 * abw].reshape(nbp, t_real, *b_shape)))
            oa = _attn_a(gp, lam_a[i], qT, kbf, vT, cfg.a_heads, lam_init)
            ob = _attn_b(gp, qT, kbf, vT, cfg.b_heads // 2, cfg.a_heads)
            hp = _out_ab(oa.reshape(-1, aw), ob.reshape(-1, bw), gates, 0, hp, gsub, w_out, cfg.a_heads,
                         1.0 - lam_init, tm)

            y, new = sample_proj(w_perm, gn, gq, groups_ab)
            nbs = gs.nb
            rows_ab_s.append((new[..., abw:abw + aw].reshape(nbs, cfg.dec_seq, *a_shape),
                              new[..., 2 * abw:2 * abw + aw].reshape(nbs, cfg.dec_seq, *a_shape),
                              new[..., abw + aw:2 * abw].reshape(nbs, cfg.dec_seq, *b_shape),
                              new[..., 2 * abw + aw:3 * abw].reshape(nbs, cfg.dec_seq, *b_shape)))
            past_k = jnp.concatenate([cache_a_k[i].reshape(nbs, cfg.past_len, aw),
                                      cache_b_k[i].reshape(nbs, cfg.past_len, bw)], axis=-1)
            past_v = jnp.concatenate([cache_a_v[i].reshape(nbs, cfg.past_len, aw),
                                      cache_b_v[i].reshape(nbs, cfg.past_len, bw)], axis=-1)
            qT = _to_qT(new[..., 0:abw], gs)
            kbf = _to_k(jnp.concatenate([past_k, new[..., abw:2 * abw]], axis=1), gs)
            vT = _to_vT(jnp.concatenate([past_v, new[..., 2 * abw:3 * abw]], axis=1), gs)
            oa = _attn_a(gs, lam_a[i], qT, kbf, vT, cfg.a_heads, lam_init)
            ob = _attn_b(gs, qT, kbf, vT, cfg.b_heads // 2, cfg.a_heads)
            hs = _out_ab(sample_rows(oa, aw), sample_rows(ob, bw), y, 3, hs, gsub, w_out, cfg.a_heads,
                         1.0 - lam_init, tm_s)
        else:
            w = w_in_c[i]
            w_pad = jnp.pad(w, ((0, 0), (0, nc_pad - w.shape[1]))).astype(BF16)
            gq = jnp.tile(g_qk_c[i], (1, 2))
            gq = jnp.pad(gq, ((0, 8 - gq.shape[0]), (0, 0)))
            w_out = w_out_c[i].astype(BF16)
            gn = g_norm_c[i][None, :]
            c_shape = (cfg.c_heads, HEAD_DIM)
            ki_col = 4 * cw + iw

            qT, kbf, vT, kv, gates, qiT, ki_bf, wiT, ki_f = _proj(hp, gn, w_pad, cos_p, sin_p, gq,
                                                                 with_sinks(groups_c, sinks_c), tm, prompt_outs(cw, extra_c))
            new = kv[:, :t_real]
            rows_c_p.append((new[..., 0:cw].reshape(nbp, t_real, *c_shape),
                             new[..., cw:2 * cw].reshape(nbp, t_real, *c_shape),
                             ki_f[:, :t_real, :HEAD_DIM]))
            oc = _dsa(gp, qiT, wiT, ki_bf, qT, kbf, vT, n_sel_p, cfg.idx_heads, cfg.c_heads // 2)
            hp = _out_c(oc.reshape(-1, cw), gates, 0, hp, w_out, tm)

            y, new = sample_proj(w_pad, gn, gq, groups_c)
            nbs = gs.nb
            ki_new = new[..., ki_col:ki_col + LANES]
            rows_c_s.append((new[..., cw:2 * cw].reshape(nbs, cfg.dec_seq, *c_shape),
                             new[..., 2 * cw:3 * cw].reshape(nbs, cfg.dec_seq, *c_shape),
                             ki_new[..., :HEAD_DIM]))
            k_all = jnp.concatenate([cache_c_k[i].reshape(nbs, cfg.past_len, cw), new[..., cw:2 * cw]], axis=1)
            v_all = jnp.concatenate([cache_c_v[i].reshape(nbs, cfg.past_len, cw), new[..., 2 * cw:3 * cw]], axis=1)
            ki_past = jnp.pad(cache_c_kidx[i], ((0, 0), (0, 0), (0, LANES - HEAD_DIM)))
            ki_all = jnp.concatenate([ki_past, jnp.where(jnp.arange(LANES) < HEAD_DIM, ki_new, 0.0)], axis=1)
            wi_new = ki_new[..., HEAD_DIM:HEAD_DIM + 8]
            wiT = jnp.pad(jnp.swapaxes(wi_new, 1, 2), ((0, 0), (0, 0), (0, gs.lq - cfg.dec_seq)))
            oc = _dsa(gs, _to_qT(new[..., 4 * cw:ki_col], gs), wiT, _to_k(ki_all, gs), _to_qT(new[..., 0:cw], gs),
                      _to_k(k_all, gs), _to_vT(v_all, gs), n_sel_s, cfg.idx_heads, cfg.c_heads // 2)
            hs = _out_c(sample_rows(oc, cw), y, 3, hs, w_out, tm_s)

    y_prompt = hp.reshape(cfg.batch, lq, d)[:, cfg.n_meta:t_real]
    y_sample = hs[:rows_s].reshape(cfg.dec_batch, cfg.dec_seq, d)

    def stack(rows, j):
        return jnp.stack([r[j] for r in rows], axis=0)

    return (y_prompt, y_sample,
            stack(rows_ab_p, 0), stack(rows_ab_p, 1), stack(rows_ab_p, 2), stack(rows_ab_p, 3),
            stack(rows_c_p, 0), stack(rows_c_p, 1), stack(rows_c_p, 2),
            stack(rows_ab_s, 0), stack(rows_ab_s, 1), stack(rows_ab_s, 2), stack(rows_ab_s, 3),
            stack(rows_c_s, 0), stack(rows_c_s, 1), stack(rows_c_s, 2))


def kernel(x_prompt, x_sample, cache_a_k, cache_a_v, cache_b_k, cache_b_v, cache_c_k, cache_c_v, cache_c_kidx,
           meta_tokens, g_norm_ab, w_in_ab, g_qk_a, lam_a, g_sub_a, w_out_ab, g_norm_c, w_in_c, g_qk_c, w_out_c):
    return _forward(Cfg(), x_prompt, x_sample, cache_a_k, cache_a_v, cache_b_k, cache_b_v, cache_c_k, cache_c_v,
                    cache_c_kidx, meta_tokens, g_norm_ab, w_in_ab, g_qk_a, lam_a, g_sub_a, w_out_ab,
                    g_norm_c, w_in_c, g_qk_c, w_out_c)
```

```python
import functools
from typing import NamedTuple

import numpy as np
import jax
import jax.numpy as jnp
from jax import lax
from jax.experimental import pallas as pl
from jax.experimental.pallas import tpu as pltpu

F32 = jnp.float32
BF16 = jnp.bfloat16
I32 = jnp.int32
I16 = jnp.int16

LANES = 128
TILE = 256
VMEM_LIMIT = 56 * 1024 * 1024

ROPE_THETA = 10000.0
EPS = 1e-6
NEG = -1e30
PAD_CHUNK = 2 ** 30
HEAD_DIM = 64
MIN16 = -2 ** 15
LOG2E = 1.4426950408889634
ONES_ROWS = 16
EXP_ZERO = -104.0


class Cfg(NamedTuple):
    d_model: int = 1024
    batch: int = 2
    seq: int = 8192
    depth: int = 4
    dec_batch: int = 8
    dec_seq: int = 16
    past_len: int = 2048
    chunk: int = 64
    n_meta: int = 16
    topk_max: int = 256
    a_heads: int = 4
    b_heads: int = 8
    c_heads: int = 16
    idx_heads: int = 4


def _round_up(x, m):
    return -(-x // m) * m


def _log2(n):
    l = int(n).bit_length() - 1
    assert (1 << l) == n, n
    return l


class Geom:
    def __init__(self, cfg, kind):
        self.kind = kind
        self.cfg = cfg
        self.sh = _log2(cfg.chunk)
        if kind == "prompt":
            self.nb = cfg.batch
            self.t_real = cfg.n_meta + cfg.seq
            self.lq = _round_up(self.t_real, 2 * TILE)
            self.lk = self.lq
            self.tq = TILE
            self.tq_sel = 2 * TILE
        else:
            self.nb = cfg.dec_batch
            self.n_keys = cfg.past_len + cfg.dec_seq
            self.lq = LANES
            self.lk = _round_up(self.n_keys, 2 * TILE)
            self.tq = LANES
            self.tq_sel = LANES
        self.tk = TILE
        self.nq = self.lq // self.tq
        self.nk = self.lk // self.tk

    def qpos(self, i):
        return i if self.kind == "prompt" else i + self.cfg.past_len

    def _cid_prompt(self, p):
        c = self.cfg
        body = lax.shift_right_logical(jnp.maximum(p - c.n_meta, 0), self.sh) + 1
        return jnp.where(p >= self.t_real, PAD_CHUNK, jnp.where(p < c.n_meta, 0, body))

    def qcid(self, pos):
        if self.kind == "prompt":
            return self._cid_prompt(pos)
        return lax.shift_right_logical(pos, self.sh)

    def kcid(self, j):
        if self.kind == "prompt":
            return self._cid_prompt(j)
        return jnp.where(j >= self.n_keys, PAD_CHUNK, lax.shift_right_logical(j, self.sh))

    def kend_cid(self, qpos):
        c = self.cfg
        if self.kind == "prompt":
            end = jnp.minimum(c.n_meta + c.chunk * self._cid_prompt(jnp.minimum(qpos, self.t_real - 1)),
                              self.t_real)
            return jnp.where(qpos >= self.t_real, self.lk, end)
        return jnp.minimum(c.chunk * (lax.shift_right_logical(qpos, self.sh) + 1), self.n_keys)

    def kend_pos(self, qpos):
        return jnp.minimum(qpos, self.lk)

    def tiles(self, q0, rule, tq=None):
        kend = self.kend_cid if rule == "cid" else self.kend_pos
        first = kend(self.qpos(q0))
        last = kend(self.qpos(q0 + (tq or self.tq) - 1))
        n_vis = (last + self.tk - 1) // self.tk
        n_full = first // self.tk
        return n_vis, n_full


def _proj_kernel(x_ref, g_ref, w_ref, cos_ref, sin_ref, gq_ref, *o_refs, groups):
    x = x_ref[...]
    ms = jnp.mean(x * x, axis=-1, keepdims=True)
    h = (x * lax.rsqrt(ms + EPS) * g_ref[...]).astype(BF16)
    cos = cos_ref[...]
    sin_s = sin_ref[...]
    lane = lax.broadcasted_iota(I32, (1, LANES), 1)
    first_half = (lane & (HEAD_DIM - 1)) < (HEAD_DIM // 2)
    r = lax.broadcasted_iota(I32, (LANES, LANES), 0)
    c = lax.broadcasted_iota(I32, (LANES, LANES), 1)
    bd = jnp.where(lax.shift_right_logical(r, 6) == lax.shift_right_logical(c, 6), 1.0, 0.0).astype(BF16)

    def rope(y):
        part = jnp.where(first_half, pltpu.roll(y, LANES - HEAD_DIM // 2, 1), pltpu.roll(y, HEAD_DIM // 2, 1))
        return y * cos + part * sin_s

    def emit(out, j, sinks):
        for (oi, mode, off) in sinks:
            o_ref = o_refs[oi]
            lo_, hi_ = off + j * LANES, off + (j + 1) * LANES
            if mode == "row":
                o_ref[:, lo_:hi_] = out.astype(o_ref.dtype)
            elif mode == "T":
                o_ref[lo_:hi_, :] = out.T.astype(o_ref.dtype)
            elif mode == "row_lo":
                o_ref[:, lo_:hi_] = jnp.where(lane < HEAD_DIM, out, 0.0).astype(o_ref.dtype)
            else:
                o_ref[...] = out.T[HEAD_DIM:HEAD_DIM + 8, :].astype(o_ref.dtype)

    for (c0, c1, kind, gi, scale, sinks) in groups:
        y = jnp.dot(h, w_ref[:, c0:c1], preferred_element_type=F32)
        for j in range((c1 - c0) // LANES):
            yb = y[:, j * LANES:(j + 1) * LANES]
            if kind == "plain":
                out = yb
            elif kind == "norm_rope":
                sq = yb * yb
                hi = sq.astype(BF16)
                lo = (sq - hi.astype(F32)).astype(BF16)
                ss = jnp.dot(hi, bd, preferred_element_type=F32) + jnp.dot(lo, bd, preferred_element_type=F32)
                yb = yb * lax.rsqrt(ss * (1.0 / HEAD_DIM) + EPS) * gq_ref[gi:gi + 1, :]
                out = rope(yb)
            elif kind == "rope":
                out = rope(yb)
            else:
                out = jnp.where(lane < HEAD_DIM, rope(yb), jnp.where(lane < HEAD_DIM + 4, yb, 0.0))
            if scale != 1.0:
                out = out * jnp.where(lane < HEAD_DIM, 1.0, scale) if kind == "kiwi" else out * scale
            emit(out, j, sinks)


def _proj(x2d, g, w_bf, cos_t, sin_t, gq, groups, tm, outs):
    rows, d = x2d.shape
    n = w_bf.shape[1]
    period = cos_t.shape[0] // tm
    res = pl.pallas_call(
        functools.partial(_proj_kernel, groups=groups),
        grid=(rows // tm,),
        in_specs=[
            pl.BlockSpec((tm, d), lambda i: (i, 0)),
            pl.BlockSpec((1, d), lambda i: (0, 0)),
            pl.BlockSpec((d, n), lambda i: (0, 0)),
            pl.BlockSpec((tm, LANES), lambda i: (i % period, 0)),
            pl.BlockSpec((tm, LANES), lambda i: (i % period, 0)),
            pl.BlockSpec(gq.shape, lambda i: (0, 0)),
        ],
        out_specs=[pl.BlockSpec(blk, imap) for (_, _, blk, imap) in outs],
        out_shape=[jax.ShapeDtypeStruct(shape, dt) for (shape, dt, _, _) in outs],
        compiler_params=pltpu.CompilerParams(dimension_semantics=("arbitrary",), vmem_limit_bytes=VMEM_LIMIT),
        name="proj",
    )(x2d, g, w_bf, cos_t, sin_t, gq)
    return res


def _row_sinks(groups):
    return tuple((c0, c1, kind, gi, scale, ((0, "row", c0),)) for (c0, c1, kind, gi, scale) in groups)


def _half_rows(qT):
    qf = qT.astype(F32)
    row = lax.broadcasted_iota(I32, qf.shape, 0)
    return (jnp.where(row < HEAD_DIM, qf, 0.0).astype(BF16), jnp.where(row >= HEAD_DIM, qf, 0.0).astype(BF16))


def _pipelined_softmax(nk, n_full, n_vis, qk, softmax, pv):
    def step(kt, slot, masked):
        qk(jnp.minimum(kt + 1, nk - 1), 1 - slot)
        pv(jnp.maximum(kt - 1, 0), 1 - slot)
        softmax(kt, slot, masked)

    def pair(masked):
        def body(j, carry):
            step(2 * j, 0, masked)
            step(2 * j + 1, 1, masked)
            return carry
        return body

    n_pairs = (n_vis + 1) // 2
    n_full_pairs = n_full // 2
    qk(0, 0)
    lax.fori_loop(0, n_full_pairs, pair(False), 0)
    lax.fori_loop(n_full_pairs, n_pairs, pair(True), 0)
    pv(2 * n_pairs - 1, 1)


def _softmax_tile(s, m_ref, al_ref, p_ref, slot, idx):
    m_old = m_ref[idx]
    m_new = jnp.maximum(m_old, jnp.max(s, axis=0, keepdims=True))
    al_ref[idx] = jnp.exp2(m_old - m_new)
    p_ref[slot, idx] = jnp.exp2(s - m_new).astype(BF16)
    m_ref[idx] = m_new


def _init_pipeline(m_s, acc_s, p_buf, al_s):
    m_s[...] = jnp.full(m_s.shape, NEG, F32)
    acc_s[...] = jnp.zeros(acc_s.shape, F32)
    p_buf[1] = jnp.zeros(p_buf.shape[1:], BF16)
    al_s[...] = jnp.ones(al_s.shape, F32)


def _tile_rows(kt, tk):
    return pl.ds(pl.multiple_of(kt * tk, tk), tk)


def _attn_a_kernel(lam_ref, q_ref, k_ref, v_ref, o_ref, m_s, acc_s, s_buf, p_buf, al_s, *, geom, lam_init):
    tq, tk = geom.tq, geom.tk
    dv = 2 * HEAD_DIM
    q0 = pl.program_id(2) * tq
    qm = _half_rows(q_ref[...])
    kend = geom.kend_cid(geom.qpos(q0 + lax.broadcasted_iota(I32, (1, tq), 1)))
    key_row = lax.broadcasted_iota(I32, (tk, 1), 0)
    n_vis, n_full = geom.tiles(q0, "cid")
    _init_pipeline(m_s, acc_s, p_buf, al_s)
    ones = jnp.ones((ONES_ROWS, tk), BF16)

    def qk(kt, slot):
        k = k_ref[_tile_rows(kt, tk), :]
        for mp in range(2):
            s_buf[slot, mp] = jnp.dot(k, qm[mp], preferred_element_type=F32)

    def pv(kt, slot):
        va = jnp.concatenate([v_ref[kt], ones], axis=0)
        for mp in range(2):
            acc_s[mp] = al_s[mp] * acc_s[mp] + jnp.dot(va, p_buf[slot, mp], preferred_element_type=F32)

    def softmax(kt, slot, masked):
        if masked:
            vis = key_row < (kend - kt * tk)
        for mp in range(2):
            s = s_buf[slot, mp]
            if masked:
                s = jnp.where(vis, s, NEG)
            _softmax_tile(s, m_s, al_s, p_buf, slot, mp)

    _pipelined_softmax(geom.nk, n_full, n_vis, qk, softmax, pv)
    lp = lam_ref[...]
    lam = (jnp.exp(jnp.sum(lp[0:1] * lp[1:2], axis=-1, keepdims=True))
           - jnp.exp(jnp.sum(lp[2:3] * lp[3:4], axis=-1, keepdims=True)) + lam_init)
    oT = acc_s[0, :dv] / acc_s[0, dv:dv + 1] - lam * (acc_s[1, :dv] / acc_s[1, dv:dv + 1])
    o_ref[...] = oT.T


def _attn_a(geom, lam_p, qT, kbf, vT, n_heads, lam_init):
    nb, tq, tk, nk = geom.nb, geom.tq, geom.tk, geom.nk
    return pl.pallas_call(
        functools.partial(_attn_a_kernel, geom=geom, lam_init=lam_init),
        grid=(nb, n_heads, geom.nq),
        in_specs=[
            pl.BlockSpec(lam_p.shape, lambda b, h, i: (0, 0)),
            pl.BlockSpec((None, LANES, tq), lambda b, h, i: (b, h, i)),
            pl.BlockSpec((None, geom.lk, LANES), lambda b, h, i: (b, 0, h)),
            pl.BlockSpec((None, nk, LANES, tk), lambda b, h, i: (b, 0, h, 0)),
        ],
        out_specs=pl.BlockSpec((None, tq, LANES), lambda b, h, i: (b, i, h)),
        out_shape=jax.ShapeDtypeStruct((nb, geom.lq, n_heads * LANES), F32),
        scratch_shapes=[pltpu.VMEM((2, 1, tq), F32), pltpu.VMEM((2, LANES + ONES_ROWS, tq), F32),
                        pltpu.VMEM((2, 2, tk, tq), F32), pltpu.VMEM((2, 2, tk, tq), BF16), pltpu.VMEM((2, 1, tq), F32)],
        compiler_params=pltpu.CompilerParams(dimension_semantics=("arbitrary",) * 3, vmem_limit_bytes=VMEM_LIMIT),
        name="attn_a",
    )(lam_p, qT, kbf, vT)


def _attn_b_kernel(q_ref, k_ref, v_ref, o_ref, c_s, acc_s, *, geom):
    tq, tk = geom.tq, geom.tk
    q0 = pl.program_id(2) * tq
    qh = _half_rows(q_ref[...])
    qpos = geom.qpos(q0 + lax.broadcasted_iota(I32, (1, tq), 1))
    n_vis, n_full = geom.tiles(q0, "pos")
    c_s[...] = jnp.zeros(c_s.shape, F32)
    acc_s[...] = jnp.zeros(acc_s.shape, F32)
    r = lax.broadcasted_iota(I32, (tk, tk), 0)
    c = lax.broadcasted_iota(I32, (tk, tk), 1)
    upper = jnp.where(c > r, 1.0, 0.0).astype(BF16)

    def body(kt, masked):
        k = k_ref[_tile_rows(kt, tk), :]
        v = v_ref[kt]
        if masked:
            vis = (kt * tk + lax.broadcasted_iota(I32, (tk, 1), 0)) < qpos
        for hh in range(2):
            z = jnp.dot(k, qh[hh], preferred_element_type=F32)
            sp = jnp.maximum(z, 0.0) + jnp.log1p(jnp.exp(-jnp.abs(z)))
            log_keep = -sp
            log_beta = z - sp
            if masked:
                log_keep = jnp.where(vis, log_keep, 0.0)
            hi = log_keep.astype(BF16)
            lo = (log_keep - hi.astype(F32)).astype(BF16)
            after = (jnp.dot(upper, hi, preferred_element_type=F32)
                     + jnp.dot(upper, lo, preferred_element_type=F32) + c_s[hh])
            a = jnp.exp(log_beta + after)
            if masked:
                a = jnp.where(vis, a, 0.0)
            vh = v[hh * HEAD_DIM:(hh + 1) * HEAD_DIM, :]
            acc_s[hh] = acc_s[hh] + jnp.dot(vh, a.astype(BF16), preferred_element_type=F32)
            c_s[hh] = c_s[hh] + jnp.sum(log_keep, axis=0, keepdims=True)

    def alive():
        return jnp.max(c_s[...]) > EXP_ZERO

    lax.fori_loop(0, n_vis - n_full, lambda i, carry: (body(n_vis - 1 - i, True), carry)[1], 0)

    def full_tile(state):
        i, _ = state
        body(n_full - 1 - i, False)
        return i + 1, alive()

    lax.while_loop(lambda state: (state[0] < n_full) & state[1], full_tile, (jnp.int32(0), alive()))
    o_ref[...] = jnp.concatenate([acc_s[0], acc_s[1]], axis=0).T


def _attn_b(geom, qT, kbf, vT, n_pairs, blk0):
    nb, tq, tk, nk = geom.nb, geom.tq, geom.tk, geom.nk
    return pl.pallas_call(
        functools.partial(_attn_b_kernel, geom=geom),
        grid=(nb, n_pairs, geom.nq),
        in_specs=[
            pl.BlockSpec((None, LANES, tq), lambda b, h, i: (b, blk0 + h, i)),
            pl.BlockSpec((None, geom.lk, LANES), lambda b, h, i: (b, 0, blk0 + h)),
            pl.BlockSpec((None, nk, LANES, tk), lambda b, h, i: (b, 0, blk0 + h, 0)),
        ],
        out_specs=pl.BlockSpec((None, tq, LANES), lambda b, h, i: (b, i, h)),
        out_shape=jax.ShapeDtypeStruct((nb, geom.lq, n_pairs * LANES), F32),
        scratch_shapes=[pltpu.VMEM((2, 1, tq), F32), pltpu.VMEM((2, HEAD_DIM, tq), F32)],
        compiler_params=pltpu.CompilerParams(dimension_semantics=("arbitrary",) * 3, vmem_limit_bytes=VMEM_LIMIT),
        name="attn_b",
    )(qT, kbf, vT)


def _select_bias(qi_ref, wi_ref, ki_ref, hi_s, lo_s, bias_s, *, geom, tq, n_sel, n_idx, q0, n_vis, n_fill):
    tk = geom.tk
    kend = geom.kend_cid(geom.qpos(q0 + lax.broadcasted_iota(I32, (1, tq), 1)))
    key_row = lax.broadcasted_iota(I32, (tk, 1), 0)
    zpad = jnp.zeros((LANES - HEAD_DIM, tq), BF16)
    qh = [jnp.concatenate([qi_ref[h * HEAD_DIM:(h + 1) * HEAD_DIM, :], zpad], axis=0) for h in range(n_idx)]
    wi = wi_ref[...]

    def vis_of(kt):
        return key_row < (kend - kt * tk)

    def rows(kt):
        return pl.ds(pl.multiple_of(kt * tk, tk), tk)

    def score_tile(kt, carry):
        ki = ki_ref[rows(kt), :]
        sc = jnp.zeros((tk, tq), F32)
        for h in range(n_idx):
            rel = jnp.maximum(jnp.dot(ki, qh[h], preferred_element_type=F32), 0.0)
            sc = sc + wi[h:h + 1, :] * rel
        sc = jnp.where(sc == 0.0, 0.0, sc)
        sc = jnp.where(vis_of(kt), sc, NEG)
        bits = lax.bitcast_convert_type(sc, I32)
        key = jnp.where(bits < 0, bits ^ jnp.int32(0x7FFFFFFF), bits)
        hi_s[rows(kt), :] = lax.shift_right_arithmetic(key, 16).astype(I16)
        lo_s[rows(kt), :] = ((key & 0xFFFF) + MIN16).astype(I16)
        return carry

    lax.fori_loop(0, n_vis, score_tile, 0)

    def pad_tile(kt, carry):
        hi_s[rows(kt), :] = jnp.full((tk, tq), MIN16, I16)
        lo_s[rows(kt), :] = jnp.full((tk, tq), MIN16, I16)
        return carry

    lax.fori_loop(n_vis, n_fill, pad_tile, 0)

    def rows2(j):
        return pl.ds(pl.multiple_of(j * 2 * tk, 2 * tk), 2 * tk)

    def count16(ref, pred_fn):
        def step(j, acc):
            hit = jnp.where(pred_fn(ref[rows2(j), :]), jnp.int16(1), jnp.int16(0))
            for t in range(2 * tk // 16):
                acc = acc + hit[t * 16:(t + 1) * 16, :]
            return acc
        acc = lax.fori_loop(0, n_fill // 2, step, jnp.zeros((16, tq), I16))
        return jnp.sum(acc.astype(I32), axis=0, keepdims=True)

    def search16(ref, target):
        def bit_step(i, prefix):
            cand = prefix | lax.shift_left(jnp.int32(1), 15 - i)
            cand_s = (cand + MIN16).astype(I16)
            cnt = count16(ref, lambda x: x >= cand_s)
            return jnp.where(cnt >= target, cand, prefix)
        return (lax.fori_loop(0, 16, bit_step, jnp.zeros((1, tq), I32)) + MIN16).astype(I16)

    hi_thr = search16(hi_s, n_sel)
    n_low = n_sel - count16(hi_s, lambda x: x > hi_thr)

    def bucket_tile(j, carry):
        lo_s[rows2(j), :] = jnp.where(hi_s[rows2(j), :] == hi_thr, lo_s[rows2(j), :], jnp.int16(MIN16))
        return carry

    lax.fori_loop(0, n_fill // 2, bucket_tile, 0)
    lo_thr = search16(lo_s, n_low)
    need = (n_low - count16(lo_s, lambda x: x > lo_thr)).astype(F32)

    r = lax.broadcasted_iota(I32, (tk, tk), 0)
    c = lax.broadcasted_iota(I32, (tk, tk), 1)
    lower = jnp.where(c < r, 1.0, 0.0).astype(BF16)

    def bias_tile(kt, run):
        hi = hi_s[rows(kt), :]
        lo = lo_s[rows(kt), :]
        same = hi == hi_thr
        gt = jnp.where((hi > hi_thr) | (same & (lo > lo_thr)), jnp.int16(1), jnp.int16(0)).astype(I32) != 0
        eq_i = jnp.where(same & (lo == lo_thr), jnp.int16(1), jnp.int16(0)).astype(I32)
        eq = eq_i.astype(F32)
        rank = jnp.dot(lower, eq.astype(BF16), preferred_element_type=F32) + run
        sel = (gt | ((eq_i != 0) & (rank < need))) & vis_of(kt)
        bias_s[rows(kt), :] = jnp.where(sel, 0.0, NEG).astype(bias_s.dtype)
        return run + jnp.sum(eq, axis=0, keepdims=True)

    lax.fori_loop(0, n_vis, bias_tile, jnp.zeros((1, tq), F32))

    def fill_tile(kt, carry):
        bias_s[rows(kt), :] = jnp.full((tk, tq), NEG, bias_s.dtype)
        return carry

    lax.fori_loop(n_vis, n_fill, fill_tile, 0)


def _dsa_kernel(qi_ref, wi_ref, ki_ref, q_ref, k_ref, v_ref, o_ref, hi_s, lo_s, bias_s, m_s, acc_s, s_buf, p_buf,
                al_s, *, geom, n_sel, n_idx):
    tq, tk, tq_sel = geom.tq, geom.tk, geom.tq_sel
    q0_sel = pl.program_id(1) * tq_sel
    n_vis_sel, _ = geom.tiles(q0_sel, "cid", tq_sel)

    @pl.when(pl.program_id(2) == 0)
    def _():
        _select_bias(qi_ref, wi_ref, ki_ref, hi_s, lo_s, bias_s, geom=geom, tq=tq_sel, n_sel=n_sel, n_idx=n_idx,
                     q0=q0_sel, n_vis=n_vis_sel, n_fill=2 * ((n_vis_sel + 1) // 2))

    ones = jnp.ones((ONES_ROWS, tk), BF16)
    for part in range(tq_sel // tq):
        cols = slice(part * tq, (part + 1) * tq)
        n_vis, _ = geom.tiles(q0_sel + part * tq, "cid")
        qh = _half_rows(q_ref[:, cols])
        _init_pipeline(m_s, acc_s, p_buf, al_s)

        def qk(kt, slot, qh=qh):
            k = k_ref[_tile_rows(kt, tk), :]
            for hh in range(2):
                s_buf[slot, hh] = jnp.dot(k, qh[hh], preferred_element_type=F32)

        def pv(kt, slot):
            v = v_ref[kt]
            for hh in range(2):
                va = jnp.concatenate([v[hh * HEAD_DIM:(hh + 1) * HEAD_DIM, :], ones], axis=0)
                acc_s[hh] = al_s[hh] * acc_s[hh] + jnp.dot(va, p_buf[slot, hh], preferred_element_type=F32)

        def softmax(kt, slot, masked, cols=cols):
            bias = bias_s[_tile_rows(kt, tk), cols].astype(F32)
            for hh in range(2):
                _softmax_tile(s_buf[slot, hh] + bias, m_s, al_s, p_buf, slot, hh)

        _pipelined_softmax(geom.nk, 0, n_vis, qk, softmax, pv)
        o_ref[cols, :] = jnp.concatenate([acc_s[0, :HEAD_DIM] / acc_s[0, HEAD_DIM:HEAD_DIM + 1],
                                          acc_s[1, :HEAD_DIM] / acc_s[1, HEAD_DIM:HEAD_DIM + 1]], axis=0).T


def _dsa(geom, qiT, wiT, ki_bf, qT, kbf, vT, n_sel, n_idx, n_pairs):
    nb, tq, tk, nk, tqs = geom.nb, geom.tq, geom.tk, geom.nk, geom.tq_sel
    return pl.pallas_call(
        functools.partial(_dsa_kernel, geom=geom, n_sel=n_sel, n_idx=n_idx),
        grid=(nb, geom.lq // tqs, n_pairs),
        in_specs=[
            pl.BlockSpec((None, n_idx * HEAD_DIM, tqs), lambda b, i, h: (b, 0, i)),
            pl.BlockSpec((None, 8, tqs), lambda b, i, h: (b, 0, i)),
            pl.BlockSpec((None, geom.lk, LANES), lambda b, i, h: (b, 0, 0)),
            pl.BlockSpec((None, LANES, tqs), lambda b, i, h: (b, h, i)),
            pl.BlockSpec((None, geom.lk, LANES), lambda b, i, h: (b, 0, h)),
            pl.BlockSpec((None, nk, LANES, tk), lambda b, i, h: (b, 0, h, 0)),
        ],
        out_specs=pl.BlockSpec((None, tqs, LANES), lambda b, i, h: (b, i, h)),
        out_shape=jax.ShapeDtypeStruct((nb, geom.lq, n_pairs * LANES), F32),
        scratch_shapes=[pltpu.VMEM((geom.lk, tqs), I16), pltpu.VMEM((geom.lk, tqs), I16), pltpu.VMEM((geom.lk, tqs), BF16),
                        pltpu.VMEM((2, 1, tq), F32), pltpu.VMEM((2, HEAD_DIM + ONES_ROWS, tq), F32),
                        pltpu.VMEM((2, 2, tk, tq), F32), pltpu.VMEM((2, 2, tk, tq), BF16), pltpu.VMEM((2, 1, tq), F32)],
        compiler_params=pltpu.CompilerParams(dimension_semantics=("arbitrary",) * 3, vmem_limit_bytes=VMEM_LIMIT),
        name="dsa",
    )(qiT, wiT, ki_bf, qT, kbf, vT)


def _silu(g):
    return g * (1.0 / (1.0 + jnp.exp(-g)))


def _out_ab_kernel(oa_ref, ob_ref, g_ref, x_ref, gsub_ref, w_ref, o_ref, *, n_heads, scale):
    wa = n_heads * LANES
    acc = x_ref[...]
    g = g_ref[...]
    for h in range(n_heads):
        blk = oa_ref[:, h * LANES:(h + 1) * LANES]
        ms = jnp.mean(blk * blk, axis=-1, keepdims=True)
        nrm = (blk * lax.rsqrt(ms + EPS) * gsub_ref[...]) * scale
        mixed = (nrm * _silu(g[:, h * LANES:(h + 1) * LANES])).astype(BF16)
        acc = acc + jnp.dot(mixed, w_ref[h * LANES:(h + 1) * LANES, :], preferred_element_type=F32)
    mixed_b = (ob_ref[...] * _silu(g[:, wa:])).astype(BF16)
    o_ref[...] = acc + jnp.dot(mixed_b, w_ref[wa:, :], preferred_element_type=F32)


def _out_ab(oa, ob, y, gcol, x2d, gsub, w_bf, n_heads, scale, tm):
    rows, d = x2d.shape
    wa, wb = oa.shape[1], ob.shape[1]
    return pl.pallas_call(
        functools.partial(_out_ab_kernel, n_heads=n_heads, scale=scale),
        grid=(rows // tm,),
        in_specs=[
            pl.BlockSpec((tm, wa), lambda i: (i, 0)),
            pl.BlockSpec((tm, wb), lambda i: (i, 0)),
            pl.BlockSpec((tm, wa + wb), lambda i: (i, gcol)),
            pl.BlockSpec((tm, d), lambda i: (i, 0)),
            pl.BlockSpec((1, LANES), lambda i: (0, 0)),
            pl.BlockSpec((wa + wb, d), lambda i: (0, 0)),
        ],
        out_specs=pl.BlockSpec((tm, d), lambda i: (i, 0)),
        out_shape=jax.ShapeDtypeStruct((rows, d), F32),
        compiler_params=pltpu.CompilerParams(dimension_semantics=("arbitrary",), vmem_limit_bytes=VMEM_LIMIT),
        name="out_ab",
    )(oa, ob, y, x2d, gsub, w_bf)


def _out_c_kernel(o_ref_in, g_ref, x_ref, w_ref, o_ref):
    mixed = (o_ref_in[...] * _silu(g_ref[...])).astype(BF16)
    o_ref[...] = x_ref[...] + jnp.dot(mixed, w_ref[...], preferred_element_type=F32)


def _out_c(o, y, gcol, x2d, w_bf, tm):
    rows, d = x2d.shape
    wc = o.shape[1]
    return pl.pallas_call(
        _out_c_kernel,
        grid=(rows // tm,),
        in_specs=[
            pl.BlockSpec((tm, wc), lambda i: (i, 0)),
            pl.BlockSpec((tm, wc), lambda i: (i, gcol)),
            pl.BlockSpec((tm, d), lambda i: (i, 0)),
            pl.BlockSpec((wc, d), lambda i: (0, 0)),
        ],
        out_specs=pl.BlockSpec((tm, d), lambda i: (i, 0)),
        out_shape=jax.ShapeDtypeStruct((rows, d), F32),
        compiler_params=pltpu.CompilerParams(dimension_semantics=("arbitrary",), vmem_limit_bytes=VMEM_LIMIT),
        name="out_c",
    )(o, y, x2d, w_bf)


def _rope_tables(pos):
    half = HEAD_DIM // 2
    inv_freq = ROPE_THETA ** (-jnp.arange(half, dtype=F32) / half)
    ang = pos.astype(F32)[:, None] * inv_freq[None, :]
    cos, sin = jnp.cos(ang), jnp.sin(ang)
    return jnp.tile(cos, (1, 4)), jnp.concatenate([-sin, sin, -sin, sin], axis=1)


def _to_qT(q, geom):
    qT = jnp.swapaxes(q.astype(BF16), 1, 2)
    return jnp.pad(qT, ((0, 0), (0, 0), (0, geom.lq - qT.shape[2])))


def _to_k(k, geom):
    k = k.astype(BF16)
    return jnp.pad(k, ((0, 0), (0, geom.lk - k.shape[1]), (0, 0)))


def _to_vT(v, geom):
    v = _to_k(v, geom)
    nb, _, w = v.shape
    return jnp.swapaxes(v.reshape(nb, geom.nk, geom.tk, w), 2, 3)


def _forward(cfg, x_prompt, x_sample, cache_a_k, cache_a_v, cache_b_k, cache_b_v, cache_c_k, cache_c_v,
             cache_c_kidx, meta_tokens, g_norm_ab, w_in_ab, g_qk_a, lam_a, g_sub_a, w_out_ab,
             g_norm_c, w_in_c, g_qk_c, w_out_c):
    d = cfg.d_model
    gp, gs = Geom(cfg, "prompt"), Geom(cfg, "sample")
    t_real = gp.t_real
    nbp, lq, tm = gp.nb, gp.lq, TILE
    tpb = lq // tm
    aw = cfg.a_heads * 2 * HEAD_DIM
    bw = cfg.b_heads * HEAD_DIM
    cw = cfg.c_heads * HEAD_DIM
    iw = cfg.idx_heads * HEAD_DIM
    abw = aw + bw
    n_sel_p = min(cfg.topk_max, cfg.seq // 4)
    n_sel_s = min(cfg.topk_max, (cfg.past_len + cfg.dec_seq) // 4)

    meta = jnp.broadcast_to(meta_tokens.astype(x_prompt.dtype)[None], (cfg.batch, cfg.n_meta, d))
    hp = jnp.concatenate([meta, x_prompt, jnp.zeros((cfg.batch, lq - t_real, d), x_prompt.dtype)], axis=1)
    hp = hp.reshape(cfg.batch * lq, d)
    rows_s = cfg.dec_batch * cfg.dec_seq
    tm_s = _round_up(rows_s, 8)
    hs = jnp.pad(x_sample.reshape(rows_s, d), ((0, tm_s - rows_s), (0, 0)))

    cos_p, sin_p = _rope_tables(jnp.arange(lq, dtype=I32))
    cos_s, sin_s = _rope_tables(cfg.past_len + jnp.arange(tm_s, dtype=I32) % cfg.dec_seq)

    s_qk = HEAD_DIM ** -0.5
    s_q2 = s_qk * LOG2E
    groups_ab = ((0, aw, "norm_rope", 0, s_q2), (aw, abw, "plain", 0, s_qk),
                 (abw, abw + aw, "norm_rope", 1, 1.0), (abw + aw, 2 * abw, "plain", 0, 1.0),
                 (2 * abw, 3 * abw, "plain", 0, 1.0), (3 * abw, 4 * abw, "plain", 0, 1.0))
    sinks_ab = (((0, "T", 0),), ((0, "T", aw),), ((1, "row", 0), (3, "row", 0)), ((1, "row", aw), (3, "row", aw)),
                ((2, "T", 0), (3, "row", abw)), ((4, "row", 0),))
    half = cw // 2
    groups_c = tuple([(j * half, (j + 1) * half, "norm_rope", 0, s_q2) for j in range(2)]
                     + [(cw + j * half, cw + (j + 1) * half, "norm_rope", 1, 1.0) for j in range(2)]
                     + [(2 * cw, 3 * cw, "plain", 0, 1.0), (3 * cw, 4 * cw, "plain", 0, 1.0),
                        (4 * cw, 4 * cw + iw, "rope", 0, s_qk),
                        (4 * cw + iw, 4 * cw + iw + LANES, "kiwi", 0, cfg.idx_heads ** -0.5)])
    sinks_c = tuple([((0, "T", j * half),) for j in range(2)]
                    + [((1, "row", j * half), (3, "row", j * half)) for j in range(2)]
                    + [((2, "T", 0), (3, "row", cw)), ((4, "row", 0),), ((5, "T", 0),),
                       ((6, "row_lo", 0), (7, "T_wi", 0), (8, "row", 0))])
    nc_pad = 4 * cw + iw + LANES

    def with_sinks(groups, sinks):
        return tuple(g + (s,) for g, s in zip(groups, sinks))

    def prompt_outs(width, extra):
        outs = [((nbp, width, lq), BF16, (None, width, tm), lambda i: (i // tpb, 0, i % tpb)),
                ((nbp, lq, width), BF16, (None, tm, width), lambda i: (i // tpb, i % tpb, 0)),
                ((nbp, gp.nk, width, tm), BF16, (None, None, width, tm), lambda i: (i // tpb, i % tpb, 0, 0)),
                ((nbp, lq, 2 * width), F32, (None, tm, 2 * width), lambda i: (i // tpb, i % tpb, 0)),
                ((nbp * lq, width), F32, (tm, width), lambda i: (i, 0))]
        return outs + extra

    extra_c = [((nbp, iw, lq), BF16, (None, iw, tm), lambda i: (i // tpb, 0, i % tpb)),
               ((nbp, lq, LANES), BF16, (None, tm, LANES), lambda i: (i // tpb, i % tpb, 0)),
               ((nbp, 8, lq), F32, (None, 8, tm), lambda i: (i // tpb, 0, i % tpb)),
               ((nbp, lq, LANES), F32, (None, tm, LANES), lambda i: (i // tpb, i % tpb, 0))]

    def sample_proj(w_bf, gn, gq, groups):
        outs = [((tm_s, w_bf.shape[1]), F32, (tm_s, w_bf.shape[1]), lambda i: (i, 0))]
        y = _proj(hs, gn, w_bf, cos_s, sin_s, gq, _row_sinks(groups), tm_s, outs)[0]
        return y, y[:rows_s].reshape(gs.nb, cfg.dec_seq, -1)

    def sample_rows(o, width):
        return jnp.pad(o[:, :cfg.dec_seq].reshape(rows_s, width), ((0, tm_s - rows_s), (0, 0)))

    rows_ab_p, rows_ab_s, rows_c_p, rows_c_s = [], [], [], []
    for l in range(cfg.depth):
        i = l // 2
        if l % 2 == 0:
            lam_init = 0.8 - 0.6 * float(np.exp(-0.3 * l))
            w = w_in_ab[i]
            wq = [w[:, j * aw:(j + 1) * aw] for j in range(4)] + [w[:, 4 * aw + j * bw:4 * aw + (j + 1) * bw] for j in range(4)]
            w_perm = jnp.concatenate([wq[0], wq[4], wq[1], wq[5], wq[2], wq[6], wq[3], wq[7]], axis=1).astype(BF16)
            gq = jnp.tile(g_qk_a[i], (1, 2))
            gq = jnp.pad(gq, ((0, 8 - gq.shape[0]), (0, 0)))
            w_out = w_out_ab[i].astype(BF16)
            gsub = g_sub_a[i][None, :]
            gn = g_norm_ab[i][None, :]
            a_shape, b_shape = (cfg.a_heads, 2 * HEAD_DIM), (cfg.b_heads, HEAD_DIM)

            qT, kbf, vT, kv, gates = _proj(hp, gn, w_perm, cos_p, sin_p, gq, with_sinks(groups_ab, sinks_ab), tm,
                                           prompt_outs(abw, []))
            new = kv[:, :t_real]
            rows_ab_p.append((new[..., 0:aw].reshape(nbp, t_real, *a_shape),
                              new[..., abw:abw + aw].reshape(nbp, t_real, *a_shape),
                              new[..., aw:abw].reshape(nbp, t_real, *b_shape),
                              new[..., abw + aw:2 * abw].reshape(nbp, t_real, *b_shape)))
            oa = _attn_a(gp, lam_a[i], qT, kbf, vT, cfg.a_heads, lam_init)
            ob = _attn_b(gp, qT, kbf, vT, cfg.b_heads // 2, cfg.a_heads)
            hp = _out_ab(oa.reshape(-1, aw), ob.reshape(-1, bw), gates, 0, hp, gsub, w_out, cfg.a_heads,
                         1.0 - lam_init, tm)

            y, new = sample_proj(w_perm, gn, gq, groups_ab)
            nbs = gs.nb
            rows_ab_s.append((new[..., abw:abw + aw].reshape(nbs, cfg.dec_seq, *a_shape),
                              new[..., 2 * abw:2 * abw + aw].reshape(nbs, cfg.dec_seq, *a_shape),
                              new[..., abw + aw:2 * abw].reshape(nbs, cfg.dec_seq, *b_shape),
                              new[..., 2 * abw + aw:3 * abw].reshape(nbs, cfg.dec_seq, *b_shape)))
            past_k = jnp.concatenate([cache_a_k[i].reshape(nbs, cfg.past_len, aw),
                                      cache_b_k[i].reshape(nbs, cfg.past_len, bw)], axis=-1)
            past_v = jnp.concatenate([cache_a_v[i].reshape(nbs, cfg.past_len, aw),
                                      cache_b_v[i].reshape(nbs, cfg.past_len, bw)], axis=-1)
            qT = _to_qT(new[..., 0:abw], gs)
            kbf = _to_k(jnp.concatenate([past_k, new[..., abw:2 * abw]], axis=1), gs)
            vT = _to_vT(jnp.concatenate([past_v, new[..., 2 * abw:3 * abw]], axis=1), gs)
            oa = _attn_a(gs, lam_a[i], qT, kbf, vT, cfg.a_heads, lam_init)
            ob = _attn_b(gs, qT, kbf, vT, cfg.b_heads // 2, cfg.a_heads)
            hs = _out_ab(sample_rows(oa, aw), sample_rows(ob, bw), y, 3, hs, gsub, w_out, cfg.a_heads,
                         1.0 - lam_init, tm_s)
        else:
            w = w_in_c[i]
            w_pad = jnp.pad(w, ((0, 0), (0, nc_pad - w.shape[1]))).astype(BF16)
            gq = jnp.tile(g_qk_c[i], (1, 2))
            gq = jnp.pad(gq, ((0, 8 - gq.shape[0]), (0, 0)))
            w_out = w_out_c[i].astype(BF16)
            gn = g_norm_c[i][None, :]
            c_shape = (cfg.c_heads, HEAD_DIM)
            ki_col = 4 * cw + iw

            qT, kbf, vT, kv, gates, qiT, ki_bf, wiT, ki_f = _proj(hp, gn, w_pad, cos_p, sin_p, gq,
                                                                 with_sinks(groups_c, sinks_c), tm, prompt_outs(cw, extra_c))
            new = kv[:, :t_real]
            rows_c_p.append((new[..., 0:cw].reshape(nbp, t_real, *c_shape),
                             new[..., cw:2 * cw].reshape(nbp, t_real, *c_shape),
                             ki_f[:, :t_real, :HEAD_DIM]))
            oc = _dsa(gp, qiT, wiT, ki_bf, qT, kbf, vT, n_sel_p, cfg.idx_heads, cfg.c_heads // 2)
            hp = _out_c(oc.reshape(-1, cw), gates, 0, hp, w_out, tm)

            y, new = sample_proj(w_pad, gn, gq, groups_c)
            nbs = gs.nb
            ki_new = new[..., ki_col:ki_col + LANES]
            rows_c_s.append((new[..., cw:2 * cw].reshape(nbs, cfg.dec_seq, *c_shape),
                             new[..., 2 * cw:3 * cw].reshape(nbs, cfg.dec_seq, *c_shape),
                             ki_new[..., :HEAD_DIM]))
            k_all = jnp.concatenate([cache_c_k[i].reshape(nbs, cfg.past_len, cw), new[..., cw:2 * cw]], axis=1)
            v_all = jnp.concatenate([cache_c_v[i].reshape(nbs, cfg.past_len, cw), new[..., 2 * cw:3 * cw]], axis=1)
            ki_past = jnp.pad(cache_c_kidx[i], ((0, 0), (0, 0), (0, LANES - HEAD_DIM)))
            ki_all = jnp.concatenate([ki_past, jnp.where(jnp.arange(LANES) < HEAD_DIM, ki_new, 0.0)], axis=1)
            wi_new = ki_new[..., HEAD_DIM:HEAD_DIM + 8]
            wiT = jnp.pad(jnp.swapaxes(wi_new, 1, 2), ((0, 0), (0, 0), (0, gs.lq - cfg.dec_seq)))
            oc = _dsa(gs, _to_qT(new[..., 4 * cw:ki_col], gs), wiT, _to_k(ki_all, gs), _to_qT(new[..., 0:cw], gs),
                      _to_k(k_all, gs), _to_vT(v_all, gs), n_sel_s, cfg.idx_heads, cfg.c_heads // 2)
            hs = _out_c(sample_rows(oc, cw), y, 3, hs, w_out, tm_s)

    y_prompt = hp.reshape(cfg.batch, lq, d)[:, cfg.n_meta:t_real]
    y_sample = hs[:rows_s].reshape(cfg.dec_batch, cfg.dec_seq, d)

    def stack(rows, j):
        return jnp.stack([r[j] for r in rows], axis=0)

    return (y_prompt, y_sample,
            stack(rows_ab_p, 0), stack(rows_ab_p, 1), stack(rows_ab_p, 2), stack(rows_ab_p, 3),
            stack(rows_c_p, 0), stack(rows_c_p, 1), stack(rows_c_p, 2),
            stack(rows_ab_s, 0), stack(rows_ab_s, 1), stack(rows_ab_s, 2), stack(rows_ab_s, 3),
            stack(rows_c_s, 0), stack(rows_c_s, 1), stack(rows_c_s, 2))


def kernel(x_prompt, x_sample, cache_a_k, cache_a_v, cache_b_k, cache_b_v, cache_c_k, cache_c_v, cache_c_kidx,
           meta_tokens, g_norm_ab, w_in_ab, g_qk_a, lam_a, g_sub_a, w_out_ab, g_norm_c, w_in_c, g_qk_c, w_out_c):
    return _forward(Cfg(), x_prompt, x_sample, cache_a_k, cache_a_v, cache_b_k, cache_b_v, cache_c_k, cache_c_v,
                    cache_c_kidx, meta_tokens, g_norm_ab, w_in_ab, g_qk_a, lam_a, g_sub_a, w_out_ab,
                    g_norm_c, w_in_c, g_qk_c, w_out_c)
```

```python
import functools
from typing import NamedTuple

import numpy as np
import jax
import jax.numpy as jnp
from jax import lax
from jax.experimental import pallas as pl
from jax.experimental.pallas import tpu as pltpu

F32 = jnp.float32
BF16 = jnp.bfloat16
I32 = jnp.int32
I16 = jnp.int16

LANES = 128
TILE = 256
VMEM_LIMIT = 56 * 1024 * 1024

ROPE_THETA = 10000.0
EPS = 1e-6
NEG = -1e30
PAD_CHUNK = 2 ** 30
HEAD_DIM = 64
MIN16 = -2 ** 15
LOG2E = 1.4426950408889634
ONES_ROWS = 16
EXP2_ZERO = -151.0


class Cfg(NamedTuple):
    d_model: int = 1024
    batch: int = 2
    seq: int = 8192
    depth: int = 4
    dec_batch: int = 8
    dec_seq: int = 16
    past_len: int = 2048
    chunk: int = 64
    n_meta: int = 16
    topk_max: int = 256
    a_heads: int = 4
    b_heads: int = 8
    c_heads: int = 16
    idx_heads: int = 4


def _round_up(x, m):
    return -(-x // m) * m


def _log2(n):
    l = int(n).bit_length() - 1
    assert (1 << l) == n, n
    return l


class Geom:
    def __init__(self, cfg, kind):
        self.kind = kind
        self.cfg = cfg
        self.sh = _log2(cfg.chunk)
        if kind == "prompt":
            self.nb = cfg.batch
            self.t_real = cfg.n_meta + cfg.seq
            self.lq = _round_up(self.t_real, 2 * TILE)
            self.lk = self.lq
            self.tq = TILE
            self.tq_sel = 2 * TILE
        else:
            self.nb = cfg.dec_batch
            self.n_keys = cfg.past_len + cfg.dec_seq
            self.lq = LANES
            self.lk = _round_up(self.n_keys, 2 * TILE)
            self.tq = LANES
            self.tq_sel = LANES
        self.tk = TILE
        self.nq = self.lq // self.tq
        self.nk = self.lk // self.tk

    def qpos(self, i):
        return i if self.kind == "prompt" else i + self.cfg.past_len

    def _cid_prompt(self, p):
        c = self.cfg
        body = lax.shift_right_logical(jnp.maximum(p - c.n_meta, 0), self.sh) + 1
        return jnp.where(p >= self.t_real, PAD_CHUNK, jnp.where(p < c.n_meta, 0, body))

    def qcid(self, pos):
        if self.kind == "prompt":
            return self._cid_prompt(pos)
        return lax.shift_right_logical(pos, self.sh)

    def kcid(self, j):
        if self.kind == "prompt":
            return self._cid_prompt(j)
        return jnp.where(j >= self.n_keys, PAD_CHUNK, lax.shift_right_logical(j, self.sh))

    def kend_cid(self, qpos):
        c = self.cfg
        if self.kind == "prompt":
            end = jnp.minimum(c.n_meta + c.chunk * self._cid_prompt(jnp.minimum(qpos, self.t_real - 1)),
                              self.t_real)
            return jnp.where(qpos >= self.t_real, self.lk, end)
        return jnp.minimum(c.chunk * (lax.shift_right_logical(qpos, self.sh) + 1), self.n_keys)

    def kend_pos(self, qpos):
        return jnp.minimum(qpos, self.lk)

    def tiles(self, q0, rule, tq=None):
        kend = self.kend_cid if rule == "cid" else self.kend_pos
        first = kend(self.qpos(q0))
        last = kend(self.qpos(q0 + (tq or self.tq) - 1))
        n_vis = (last + self.tk - 1) // self.tk
        n_full = first // self.tk
        return n_vis, n_full


def _proj_kernel(x_ref, g_ref, w_ref, cos_ref, sin_ref, gq_ref, *o_refs, groups):
    x = x_ref[...]
    ms = jnp.mean(x * x, axis=-1, keepdims=True)
    h = (x * lax.rsqrt(ms + EPS) * g_ref[...]).astype(BF16)
    cos = cos_ref[...]
    sin_s = sin_ref[...]
    lane = lax.broadcasted_iota(I32, (1, LANES), 1)
    first_half = (lane & (HEAD_DIM - 1)) < (HEAD_DIM // 2)
    r = lax.broadcasted_iota(I32, (LANES, LANES), 0)
    c = lax.broadcasted_iota(I32, (LANES, LANES), 1)
    bd = jnp.where(lax.shift_right_logical(r, 6) == lax.shift_right_logical(c, 6), 1.0, 0.0).astype(BF16)

    def rope(y):
        part = jnp.where(first_half, pltpu.roll(y, LANES - HEAD_DIM // 2, 1), pltpu.roll(y, HEAD_DIM // 2, 1))
        return y * cos + part * sin_s

    def emit(out, j, sinks):
        for (oi, mode, off) in sinks:
            o_ref = o_refs[oi]
            lo_, hi_ = off + j * LANES, off + (j + 1) * LANES
            if mode == "row":
                o_ref[:, lo_:hi_] = out.astype(o_ref.dtype)
            elif mode == "T":
                o_ref[lo_:hi_, :] = out.T.astype(o_ref.dtype)
            elif mode == "row_lo":
                o_ref[:, lo_:hi_] = jnp.where(lane < HEAD_DIM, out, 0.0).astype(o_ref.dtype)
            else:
                o_ref[...] = out.T[HEAD_DIM:HEAD_DIM + 8, :].astype(o_ref.dtype)

    for (c0, c1, kind, gi, scale, sinks) in groups:
        y = jnp.dot(h, w_ref[:, c0:c1], preferred_element_type=F32)
        for j in range((c1 - c0) // LANES):
            yb = y[:, j * LANES:(j + 1) * LANES]
            if kind == "plain":
                out = yb
            elif kind == "norm_rope":
                sq = yb * yb
                hi = sq.astype(BF16)
                lo = (sq - hi.astype(F32)).astype(BF16)
                ss = jnp.dot(hi, bd, preferred_element_type=F32) + jnp.dot(lo, bd, preferred_element_type=F32)
                yb = yb * lax.rsqrt(ss * (1.0 / HEAD_DIM) + EPS) * gq_ref[gi:gi + 1, :]
                out = rope(yb)
            elif kind == "rope":
                out = rope(yb)
            else:
                out = jnp.where(lane < HEAD_DIM, rope(yb), jnp.where(lane < HEAD_DIM + 4, yb, 0.0))
            if scale != 1.0:
                out = out * jnp.where(lane < HEAD_DIM, 1.0, scale) if kind == "kiwi" else out * scale
            emit(out, j, sinks)


def _proj(x2d, g, w_bf, cos_t, sin_t, gq, groups, tm, outs):
    rows, d = x2d.shape
    n = w_bf.shape[1]
    period = cos_t.shape[0] // tm
    res = pl.pallas_call(
        functools.partial(_proj_kernel, groups=groups),
        grid=(rows // tm,),
        in_specs=[
            pl.BlockSpec((tm, d), lambda i: (i, 0)),
            pl.BlockSpec((1, d), lambda i: (0, 0)),
            pl.BlockSpec((d, n), lambda i: (0, 0)),
            pl.BlockSpec((tm, LANES), lambda i: (i % period, 0)),
            pl.BlockSpec((tm, LANES), lambda i: (i % period, 0)),
            pl.BlockSpec(gq.shape, lambda i: (0, 0)),
        ],
        out_specs=[pl.BlockSpec(blk, imap) for (_, _, blk, imap) in outs],
        out_shape=[jax.ShapeDtypeStruct(shape, dt) for (shape, dt, _, _) in outs],
        compiler_params=pltpu.CompilerParams(dimension_semantics=("arbitrary",), vmem_limit_bytes=VMEM_LIMIT),
        name="proj",
    )(x2d, g, w_bf, cos_t, sin_t, gq)
    return res


def _row_sinks(groups):
    return tuple((c0, c1, kind, gi, scale, ((0, "row", c0),)) for (c0, c1, kind, gi, scale) in groups)


def _half_rows(qT):
    qf = qT.astype(F32)
    row = lax.broadcasted_iota(I32, qf.shape, 0)
    return (jnp.where(row < HEAD_DIM, qf, 0.0).astype(BF16), jnp.where(row >= HEAD_DIM, qf, 0.0).astype(BF16))


def _pipelined_softmax(nk, n_full, n_vis, qk, softmax, pv):
    def step(kt, slot, masked):
        qk(jnp.minimum(kt + 1, nk - 1), 1 - slot)
        pv(jnp.maximum(kt - 1, 0), 1 - slot)
        softmax(kt, slot, masked)

    def pair(masked):
        def body(j, carry):
            step(2 * j, 0, masked)
            step(2 * j + 1, 1, masked)
            return carry
        return body

    n_pairs = (n_vis + 1) // 2
    n_full_pairs = n_full // 2
    qk(0, 0)
    lax.fori_loop(0, n_full_pairs, pair(False), 0)
    lax.fori_loop(n_full_pairs, n_pairs, pair(True), 0)
    pv(2 * n_pairs - 1, 1)


def _softmax_tile(s, m_ref, al_ref, p_ref, slot, idx):
    m_old = m_ref[idx]
    m_new = jnp.maximum(m_old, jnp.max(s, axis=0, keepdims=True))
    al_ref[idx] = jnp.exp2(m_old - m_new)
    p_ref[slot, idx] = jnp.exp2(s - m_new).astype(BF16)
    m_ref[idx] = m_new


def _init_pipeline(m_s, acc_s, p_buf, al_s):
    m_s[...] = jnp.full(m_s.shape, NEG, F32)
    acc_s[...] = jnp.zeros(acc_s.shape, F32)
    p_buf[1] = jnp.zeros(p_buf.shape[1:], BF16)
    al_s[...] = jnp.ones(al_s.shape, F32)


def _tile_rows(kt, tk):
    return pl.ds(pl.multiple_of(kt * tk, tk), tk)


def _attn_a_kernel(lam_ref, q_ref, k_ref, v_ref, o_ref, m_s, acc_s, s_buf, p_buf, al_s, *, geom, lam_init):
    tq, tk = geom.tq, geom.tk
    dv = 2 * HEAD_DIM
    q0 = pl.program_id(2) * tq
    qm = _half_rows(q_ref[...])
    kend = geom.kend_cid(geom.qpos(q0 + lax.broadcasted_iota(I32, (1, tq), 1)))
    key_row = lax.broadcasted_iota(I32, (tk, 1), 0)
    n_vis, n_full = geom.tiles(q0, "cid")
    _init_pipeline(m_s, acc_s, p_buf, al_s)
    ones = jnp.ones((ONES_ROWS, tk), BF16)

    def qk(kt, slot):
        k = k_ref[_tile_rows(kt, tk), :]
        for mp in range(2):
            s_buf[slot, mp] = jnp.dot(k, qm[mp], preferred_element_type=F32)

    def pv(kt, slot):
        va = jnp.concatenate([v_ref[kt], ones], axis=0)
        for mp in range(2):
            acc_s[mp] = al_s[mp] * acc_s[mp] + jnp.dot(va, p_buf[slot, mp], preferred_element_type=F32)

    def softmax(kt, slot, masked):
        if masked:
            vis = key_row < (kend - kt * tk)
        for mp in range(2):
            s = s_buf[slot, mp]
            if masked:
                s = jnp.where(vis, s, NEG)
            _softmax_tile(s, m_s, al_s, p_buf, slot, mp)

    _pipelined_softmax(geom.nk, n_full, n_vis, qk, softmax, pv)
    lp = lam_ref[...]
    lam = (jnp.exp(jnp.sum(lp[0:1] * lp[1:2], axis=-1, keepdims=True))
           - jnp.exp(jnp.sum(lp[2:3] * lp[3:4], axis=-1, keepdims=True)) + lam_init)
    oT = acc_s[0, :dv] / acc_s[0, dv:dv + 1] - lam * (acc_s[1, :dv] / acc_s[1, dv:dv + 1])
    o_ref[...] = oT.T


def _attn_a(geom, lam_p, qT, kbf, vT, n_heads, lam_init):
    nb, tq, tk, nk = geom.nb, geom.tq, geom.tk, geom.nk
    return pl.pallas_call(
        functools.partial(_attn_a_kernel, geom=geom, lam_init=lam_init),
        grid=(nb, n_heads, geom.nq),
        in_specs=[
            pl.BlockSpec(lam_p.shape, lambda b, h, i: (0, 0)),
            pl.BlockSpec((None, LANES, tq), lambda b, h, i: (b, h, i)),
            pl.BlockSpec((None, geom.lk, LANES), lambda b, h, i: (b, 0, h)),
            pl.BlockSpec((None, nk, LANES, tk), lambda b, h, i: (b, 0, h, 0)),
        ],
        out_specs=pl.BlockSpec((None, tq, LANES), lambda b, h, i: (b, i, h)),
        out_shape=jax.ShapeDtypeStruct((nb, geom.lq, n_heads * LANES), F32),
        scratch_shapes=[pltpu.VMEM((2, 1, tq), F32), pltpu.VMEM((2, LANES + ONES_ROWS, tq), F32),
                        pltpu.VMEM((2, 2, tk, tq), F32), pltpu.VMEM((2, 2, tk, tq), BF16), pltpu.VMEM((2, 1, tq), F32)],
        compiler_params=pltpu.CompilerParams(dimension_semantics=("arbitrary",) * 3, vmem_limit_bytes=VMEM_LIMIT),
        name="attn_a",
    )(lam_p, qT, kbf, vT)


def _attn_b_kernel(q_ref, k_ref, v_ref, o_ref, c_s, acc_s, z_buf, hi_buf, lo_buf, lb_buf, a_buf, *, geom, n_heads):
    tq, tk = geom.tq, geom.tk
    q0 = pl.program_id(1) * tq
    qh = []
    for p in range(n_heads // 2):
        qh.extend(_half_rows(q_ref[p * LANES:(p + 1) * LANES, :]))
    qpos = geom.qpos(q0 + lax.broadcasted_iota(I32, (1, tq), 1))
    key_row = lax.broadcasted_iota(I32, (tk, 1), 0)
    n_vis, n_full = geom.tiles(q0, "pos")
    c_s[...] = jnp.zeros(c_s.shape, F32)
    acc_s[...] = jnp.zeros(acc_s.shape, F32)
    r = lax.broadcasted_iota(I32, (tk, tk), 0)
    c = lax.broadcasted_iota(I32, (tk, tk), 1)
    upper = jnp.where(c > r, 1.0, 0.0).astype(BF16)

    def body(kt, masked):
        k = k_ref[_tile_rows(kt, tk), :]
        v = v_ref[kt]
        if masked:
            vis = key_row < (qpos - kt * tk)
        for h in range(n_heads):
            p = h // 2
            z_buf[h] = jnp.dot(k[:, p * LANES:(p + 1) * LANES], qh[h], preferred_element_type=F32)
        sums = []
        for h in range(n_heads):
            z = z_buf[h]
            sp = jnp.maximum(z, 0.0) + jnp.log2(1.0 + jnp.exp2(-jnp.abs(z)))
            log_keep = -sp
            if masked:
                log_keep = jnp.where(vis, log_keep, 0.0)
            hi = log_keep.astype(BF16)
            hi_buf[h] = hi
            lo_buf[h] = (log_keep - hi.astype(F32)).astype(BF16)
            lb_buf[h] = z - sp
            sums.append(jnp.sum(log_keep, axis=0, keepdims=True))
        for h in range(n_heads):
            z_buf[h] = (jnp.dot(upper, hi_buf[h], preferred_element_type=F32)
                        + jnp.dot(upper, lo_buf[h], preferred_element_type=F32))
        for h in range(n_heads):
            a = jnp.exp2(lb_buf[h] + z_buf[h] + c_s[h])
            if masked:
                a = jnp.where(vis, a, 0.0)
            a_buf[h] = a.astype(BF16)
        for h in range(n_heads):
            acc_s[h] = acc_s[h] + jnp.dot(v[h * HEAD_DIM:(h + 1) * HEAD_DIM, :], a_buf[h], preferred_element_type=F32)
            c_s[h] = c_s[h] + sums[h]

    def alive():
        return jnp.max(c_s[...]) > EXP2_ZERO

    lax.fori_loop(0, n_vis - n_full, lambda i, carry: (body(n_vis - 1 - i, True), carry)[1], 0)

    def full_tile(state):
        i, _ = state
        body(n_full - 1 - i, False)
        return i + 1, alive()

    lax.while_loop(lambda state: (state[0] < n_full) & state[1], full_tile, (jnp.int32(0), alive()))
    for p in range(n_heads // 2):
        o_ref[:, p * LANES:(p + 1) * LANES] = jnp.concatenate([acc_s[2 * p], acc_s[2 * p + 1]], axis=0).T


def _attn_b(geom, qT, kbf, vT, n_pairs, blk0):
    nb, tq, tk, nk = geom.nb, geom.tq, geom.tk, geom.nk
    w = n_pairs * LANES
    assert (blk0 * LANES) % w == 0
    wb = blk0 * LANES // w
    n_heads = 2 * n_pairs
    return pl.pallas_call(
        functools.partial(_attn_b_kernel, geom=geom, n_heads=n_heads),
        grid=(nb, geom.nq),
        in_specs=[
            pl.BlockSpec((None, w, tq), lambda b, i: (b, wb, i)),
            pl.BlockSpec((None, geom.lk, w), lambda b, i: (b, 0, wb)),
            pl.BlockSpec((None, nk, w, tk), lambda b, i: (b, 0, wb, 0)),
        ],
        out_specs=pl.BlockSpec((None, tq, w), lambda b, i: (b, i, 0)),
        out_shape=jax.ShapeDtypeStruct((nb, geom.lq, w), F32),
        scratch_shapes=[pltpu.VMEM((n_heads, 1, tq), F32), pltpu.VMEM((n_heads, HEAD_DIM, tq), F32),
                        pltpu.VMEM((n_heads, tk, tq), F32), pltpu.VMEM((n_heads, tk, tq), BF16),
                        pltpu.VMEM((n_heads, tk, tq), BF16), pltpu.VMEM((n_heads, tk, tq), F32),
                        pltpu.VMEM((n_heads, tk, tq), BF16)],
        compiler_params=pltpu.CompilerParams(dimension_semantics=("arbitrary",) * 2, vmem_limit_bytes=VMEM_LIMIT),
        name="attn_b",
    )(qT, kbf, vT)


def _select_bias(qi_ref, wi_ref, ki_ref, hi_s, lo_s, bias_s, *, geom, tq, n_sel, n_idx, q0, n_vis, n_fill):
    tk = geom.tk
    kend = geom.kend_cid(geom.qpos(q0 + lax.broadcasted_iota(I32, (1, tq), 1)))
    key_row = lax.broadcasted_iota(I32, (tk, 1), 0)
    zpad = jnp.zeros((LANES - HEAD_DIM, tq), BF16)
    qh = [jnp.concatenate([qi_ref[h * HEAD_DIM:(h + 1) * HEAD_DIM, :], zpad], axis=0) for h in range(n_idx)]
    wi = wi_ref[...]

    def vis_of(kt):
        return key_row < (kend - kt * tk)

    def rows(kt):
        return pl.ds(pl.multiple_of(kt * tk, tk), tk)

    def score_tile(kt, carry):
        ki = ki_ref[rows(kt), :]
        sc = jnp.zeros((tk, tq), F32)
        for h in range(n_idx):
            rel = jnp.maximum(jnp.dot(ki, qh[h], preferred_element_type=F32), 0.0)
            sc = sc + wi[h:h + 1, :] * rel
        sc = jnp.where(sc == 0.0, 0.0, sc)
        sc = jnp.where(vis_of(kt), sc, NEG)
        bits = lax.bitcast_convert_type(sc, I32)
        key = jnp.where(bits < 0, bits ^ jnp.int32(0x7FFFFFFF), bits)
        hi_s[rows(kt), :] = lax.shift_right_arithmetic(key, 16).astype(I16)
        lo_s[rows(kt), :] = ((key & 0xFFFF) + MIN16).astype(I16)
        return carry

    lax.fori_loop(0, n_vis, score_tile, 0)

    def pad_tile(kt, carry):
        hi_s[rows(kt), :] = jnp.full((tk, tq), MIN16, I16)
        lo_s[rows(kt), :] = jnp.full((tk, tq), MIN16, I16)
        return carry

    lax.fori_loop(n_vis, n_fill, pad_tile, 0)

    def rows2(j):
        return pl.ds(pl.multiple_of(j * 2 * tk, 2 * tk), 2 * tk)

    def count16(ref, pred_fn):
        def step(j, acc):
            hit = jnp.where(pred_fn(ref[rows2(j), :]), jnp.int16(1), jnp.int16(0))
            for t in range(2 * tk // 16):
                acc = acc + hit[t * 16:(t + 1) * 16, :]
            return acc
        acc = lax.fori_loop(0, n_fill // 2, step, jnp.zeros((16, tq), I16))
        return jnp.sum(acc.astype(I32), axis=0, keepdims=True)

    def search16(ref, target):
        def bit_step(i, prefix):
            cand = prefix | lax.shift_left(jnp.int32(1), 15 - i)
            cand_s = (cand + MIN16).astype(I16)
            cnt = count16(ref, lambda x: x >= cand_s)
            return jnp.where(cnt >= target, cand, prefix)
        return (lax.fori_loop(0, 16, bit_step, jnp.zeros((1, tq), I32)) + MIN16).astype(I16)

    hi_thr = search16(hi_s, n_sel)
    n_low = n_sel - count16(hi_s, lambda x: x > hi_thr)

    def bucket_tile(j, carry):
        lo_s[rows2(j), :] = jnp.where(hi_s[rows2(j), :] == hi_thr, lo_s[rows2(j), :], jnp.int16(MIN16))
        return carry

    lax.fori_loop(0, n_fill // 2, bucket_tile, 0)
    lo_thr = search16(lo_s, n_low)
    need = (n_low - count16(lo_s, lambda x: x > lo_thr)).astype(F32)

    r = lax.broadcasted_iota(I32, (tk, tk), 0)
    c = lax.broadcasted_iota(I32, (tk, tk), 1)
    lower = jnp.where(c < r, 1.0, 0.0).astype(BF16)

    def bias_tile(kt, run):
        hi = hi_s[rows(kt), :]
        lo = lo_s[rows(kt), :]
        same = hi == hi_thr
        gt = jnp.where((hi > hi_thr) | (same & (lo > lo_thr)), jnp.int16(1), jnp.int16(0)).astype(I32) != 0
        eq_i = jnp.where(same & (lo == lo_thr), jnp.int16(1), jnp.int16(0)).astype(I32)
        eq = eq_i.astype(F32)
        rank = jnp.dot(lower, eq.astype(BF16), preferred_element_type=F32) + run
        sel = (gt | ((eq_i != 0) & (rank < need))) & vis_of(kt)
        bias_s[rows(kt), :] = jnp.where(sel, 0.0, NEG).astype(bias_s.dtype)
        return run + jnp.sum(eq, axis=0, keepdims=True)

    lax.fori_loop(0, n_vis, bias_tile, jnp.zeros((1, tq), F32))

    def fill_tile(kt, carry):
        bias_s[rows(kt), :] = jnp.full((tk, tq), NEG, bias_s.dtype)
        return carry

    lax.fori_loop(n_vis, n_fill, fill_tile, 0)


def _dsa_kernel(qi_ref, wi_ref, ki_ref, q_ref, k_ref, v_ref, o_ref, hi_s, lo_s, bias_s, m_s, acc_s, s_buf, p_buf,
                al_s, *, geom, n_sel, n_idx):
    tq, tk, tq_sel = geom.tq, geom.tk, geom.tq_sel
    q0_sel = pl.program_id(1) * tq_sel
    n_vis_sel, _ = geom.tiles(q0_sel, "cid", tq_sel)

    @pl.when(pl.program_id(2) == 0)
    def _():
        _select_bias(qi_ref, wi_ref, ki_ref, hi_s, lo_s, bias_s, geom=geom, tq=tq_sel, n_sel=n_sel, n_idx=n_idx,
                     q0=q0_sel, n_vis=n_vis_sel, n_fill=2 * ((n_vis_sel + 1) // 2))

    ones = jnp.ones((ONES_ROWS, tk), BF16)
    for part in range(tq_sel // tq):
        cols = slice(part * tq, (part + 1) * tq)
        n_vis, _ = geom.tiles(q0_sel + part * tq, "cid")
        qh = _half_rows(q_ref[:, cols])
        _init_pipeline(m_s, acc_s, p_buf, al_s)

        def qk(kt, slot, qh=qh):
            k = k_ref[_tile_rows(kt, tk), :]
            for hh in range(2):
                s_buf[slot, hh] = jnp.dot(k, qh[hh], preferred_element_type=F32)

        def pv(kt, slot):
            v = v_ref[kt]
            for hh in range(2):
                va = jnp.concatenate([v[hh * HEAD_DIM:(hh + 1) * HEAD_DIM, :], ones], axis=0)
                acc_s[hh] = al_s[hh] * acc_s[hh] + jnp.dot(va, p_buf[slot, hh], preferred_element_type=F32)

        def softmax(kt, slot, masked, cols=cols):
            bias = bias_s[_tile_rows(kt, tk), cols].astype(F32)
            for hh in range(2):
                _softmax_tile(s_buf[slot, hh] + bias, m_s, al_s, p_buf, slot, hh)

        _pipelined_softmax(geom.nk, 0, n_vis, qk, softmax, pv)
        o_ref[cols, :] = jnp.concatenate([acc_s[0, :HEAD_DIM] / acc_s[0, HEAD_DIM:HEAD_DIM + 1],
                                          acc_s[1, :HEAD_DIM] / acc_s[1, HEAD_DIM:HEAD_DIM + 1]], axis=0).T


def _dsa(geom, qiT, wiT, ki_bf, qT, kbf, vT, n_sel, n_idx, n_pairs):
    nb, tq, tk, nk, tqs = geom.nb, geom.tq, geom.tk, geom.nk, geom.tq_sel
    return pl.pallas_call(
        functools.partial(_dsa_kernel, geom=geom, n_sel=n_sel, n_idx=n_idx),
        grid=(nb, geom.lq // tqs, n_pairs),
        in_specs=[
            pl.BlockSpec((None, n_idx * HEAD_DIM, tqs), lambda b, i, h: (b, 0, i)),
            pl.BlockSpec((None, 8, tqs), lambda b, i, h: (b, 0, i)),
            pl.BlockSpec((None, geom.lk, LANES), lambda b, i, h: (b, 0, 0)),
            pl.BlockSpec((None, LANES, tqs), lambda b, i, h: (b, h, i)),
            pl.BlockSpec((None, geom.lk, LANES), lambda b, i, h: (b, 0, h)),
            pl.BlockSpec((None, nk, LANES, tk), lambda b, i, h: (b, 0, h, 0)),
        ],
        out_specs=pl.BlockSpec((None, tqs, LANES), lambda b, i, h: (b, i, h)),
        out_shape=jax.ShapeDtypeStruct((nb, geom.lq, n_pairs * LANES), F32),
        scratch_shapes=[pltpu.VMEM((geom.lk, tqs), I16), pltpu.VMEM((geom.lk, tqs), I16), pltpu.VMEM((geom.lk, tqs), BF16),
                        pltpu.VMEM((2, 1, tq), F32), pltpu.VMEM((2, HEAD_DIM + ONES_ROWS, tq), F32),
                        pltpu.VMEM((2, 2, tk, tq), F32), pltpu.VMEM((2, 2, tk, tq), BF16), pltpu.VMEM((2, 1, tq), F32)],
        compiler_params=pltpu.CompilerParams(dimension_semantics=("arbitrary",) * 3, vmem_limit_bytes=VMEM_LIMIT),
        name="dsa",
    )(qiT, wiT, ki_bf, qT, kbf, vT)


def _silu(g):
    return g * (1.0 / (1.0 + jnp.exp(-g)))


def _out_ab_kernel(oa_ref, ob_ref, g_ref, x_ref, gsub_ref, w_ref, o_ref, *, n_heads, scale):
    wa = n_heads * LANES
    acc = x_ref[...]
    g = g_ref[...]
    for h in range(n_heads):
        blk = oa_ref[:, h * LANES:(h + 1) * LANES]
        ms = jnp.mean(blk * blk, axis=-1, keepdims=True)
        nrm = (blk * lax.rsqrt(ms + EPS) * gsub_ref[...]) * scale
        mixed = (nrm * _silu(g[:, h * LANES:(h + 1) * LANES])).astype(BF16)
        acc = acc + jnp.dot(mixed, w_ref[h * LANES:(h + 1) * LANES, :], preferred_element_type=F32)
    mixed_b = (ob_ref[...] * _silu(g[:, wa:])).astype(BF16)
    o_ref[...] = acc + jnp.dot(mixed_b, w_ref[wa:, :], preferred_element_type=F32)


def _out_ab(oa, ob, y, gcol, x2d, gsub, w_bf, n_heads, scale, tm):
    rows, d = x2d.shape
    wa, wb = oa.shape[1], ob.shape[1]
    return pl.pallas_call(
        functools.partial(_out_ab_kernel, n_heads=n_heads, scale=scale),
        grid=(rows // tm,),
        in_specs=[
            pl.BlockSpec((tm, wa), lambda i: (i, 0)),
            pl.BlockSpec((tm, wb), lambda i: (i, 0)),
            pl.BlockSpec((tm, wa + wb), lambda i: (i, gcol)),
            pl.BlockSpec((tm, d), lambda i: (i, 0)),
            pl.BlockSpec((1, LANES), lambda i: (0, 0)),
            pl.BlockSpec((wa + wb, d), lambda i: (0, 0)),
        ],
        out_specs=pl.BlockSpec((tm, d), lambda i: (i, 0)),
        out_shape=jax.ShapeDtypeStruct((rows, d), F32),
        compiler_params=pltpu.CompilerParams(dimension_semantics=("arbitrary",), vmem_limit_bytes=VMEM_LIMIT),
        name="out_ab",
    )(oa, ob, y, x2d, gsub, w_bf)


def _out_c_kernel(o_ref_in, g_ref, x_ref, w_ref, o_ref):
    mixed = (o_ref_in[...] * _silu(g_ref[...])).astype(BF16)
    o_ref[...] = x_ref[...] + jnp.dot(mixed, w_ref[...], preferred_element_type=F32)


def _out_c(o, y, gcol, x2d, w_bf, tm):
    rows, d = x2d.shape
    wc = o.shape[1]
    return pl.pallas_call(
        _out_c_kernel,
        grid=(rows // tm,),
        in_specs=[
            pl.BlockSpec((tm, wc), lambda i: (i, 0)),
            pl.BlockSpec((tm, wc), lambda i: (i, gcol)),
            pl.BlockSpec((tm, d), lambda i: (i, 0)),
            pl.BlockSpec((wc, d), lambda i: (0, 0)),
        ],
        out_specs=pl.BlockSpec((tm, d), lambda i: (i, 0)),
        out_shape=jax.ShapeDtypeStruct((rows, d), F32),
        compiler_params=pltpu.CompilerParams(dimension_semantics=("arbitrary",), vmem_limit_bytes=VMEM_LIMIT),
        name="out_c",
    )(o, y, x2d, w_bf)


def _rope_tables(pos):
    half = HEAD_DIM // 2
    inv_freq = ROPE_THETA ** (-jnp.arange(half, dtype=F32) / half)
    ang = pos.astype(F32)[:, None] * inv_freq[None, :]
    cos, sin = jnp.cos(ang), jnp.sin(ang)
    return jnp.tile(cos, (1, 4)), jnp.concatenate([-sin, sin, -sin, sin], axis=1)


def _to_qT(q, geom):
    qT = jnp.swapaxes(q.astype(BF16), 1, 2)
    return jnp.pad(qT, ((0, 0), (0, 0), (0, geom.lq - qT.shape[2])))


def _to_k(k, geom):
    k = k.astype(BF16)
    return jnp.pad(k, ((0, 0), (0, geom.lk - k.shape[1]), (0, 0)))


def _to_vT(v, geom):
    v = _to_k(v, geom)
    nb, _, w = v.shape
    return jnp.swapaxes(v.reshape(nb, geom.nk, geom.tk, w), 2, 3)


def _forward(cfg, x_prompt, x_sample, cache_a_k, cache_a_v, cache_b_k, cache_b_v, cache_c_k, cache_c_v,
             cache_c_kidx, meta_tokens, g_norm_ab, w_in_ab, g_qk_a, lam_a, g_sub_a, w_out_ab,
             g_norm_c, w_in_c, g_qk_c, w_out_c):
    d = cfg.d_model
    gp, gs = Geom(cfg, "prompt"), Geom(cfg, "sample")
    t_real = gp.t_real
    nbp, lq, tm = gp.nb, gp.lq, TILE
    tpb = lq // tm
    aw = cfg.a_heads * 2 * HEAD_DIM
    bw = cfg.b_heads * HEAD_DIM
    cw = cfg.c_heads * HEAD_DIM
    iw = cfg.idx_heads * HEAD_DIM
    abw = aw + bw
    n_sel_p = min(cfg.topk_max, cfg.seq // 4)
    n_sel_s = min(cfg.topk_max, (cfg.past_len + cfg.dec_seq) // 4)

    meta = jnp.broadcast_to(meta_tokens.astype(x_prompt.dtype)[None], (cfg.batch, cfg.n_meta, d))
    hp = jnp.concatenate([meta, x_prompt, jnp.zeros((cfg.batch, lq - t_real, d), x_prompt.dtype)], axis=1)
    hp = hp.reshape(cfg.batch * lq, d)
    rows_s = cfg.dec_batch * cfg.dec_seq
    tm_s = _round_up(rows_s, 8)
    hs = jnp.pad(x_sample.reshape(rows_s, d), ((0, tm_s - rows_s), (0, 0)))

    cos_p, sin_p = _rope_tables(jnp.arange(lq, dtype=I32))
    cos_s, sin_s = _rope_tables(cfg.past_len + jnp.arange(tm_s, dtype=I32) % cfg.dec_seq)

    s_qk = HEAD_DIM ** -0.5
    s_q2 = s_qk * LOG2E
    groups_ab = ((0, aw, "norm_rope", 0, s_q2), (aw, abw, "plain", 0, s_q2),
                 (abw, abw + aw, "norm_rope", 1, 1.0), (abw + aw, 2 * abw, "plain", 0, 1.0),
                 (2 * abw, 3 * abw, "plain", 0, 1.0), (3 * abw, 4 * abw, "plain", 0, 1.0))
    sinks_ab = (((0, "T", 0),), ((0, "T", aw),), ((1, "row", 0), (3, "row", 0)), ((1, "row", aw), (3, "row", aw)),
                ((2, "T", 0), (3, "row", abw)), ((4, "row", 0),))
    half = cw // 2
    groups_c = tuple([(j * half, (j + 1) * half, "norm_rope", 0, s_q2) for j in range(2)]
                     + [(cw + j * half, cw + (j + 1) * half, "norm_rope", 1, 1.0) for j in range(2)]
                     + [(2 * cw, 3 * cw, "plain", 0, 1.0), (3 * cw, 4 * cw, "plain", 0, 1.0),
                        (4 * cw, 4 * cw + iw, "rope", 0, s_qk),
                        (4 * cw + iw, 4 * cw + iw + LANES, "kiwi", 0, cfg.idx_heads ** -0.5)])
    sinks_c = tuple([((0, "T", j * half),) for j in range(2)]
                    + [((1, "row", j * half), (3, "row", j * half)) for j in range(2)]
                    + [((2, "T", 0), (3, "row", cw)), ((4, "row", 0),), ((5, "T", 0),),
                       ((6, "row_lo", 0), (7, "T_wi", 0), (8, "row", 0))])
    nc_pad = 4 * cw + iw + LANES

    def with_sinks(groups, sinks):
        return tuple(g + (s,) for g, s in zip(groups, sinks))

    def prompt_outs(width, extra):
        outs = [((nbp, width, lq), BF16, (None, width, tm), lambda i: (i // tpb, 0, i % tpb)),
                ((nbp, lq, width), BF16, (None, tm, width), lambda i: (i // tpb, i % tpb, 0)),
                ((nbp, gp.nk, width, tm), BF16, (None, None, width, tm), lambda i: (i // tpb, i % tpb, 0, 0)),
                ((nbp, lq, 2 * width), F32, (None, tm, 2 * width), lambda i: (i // tpb, i % tpb, 0)),
                ((nbp * lq, width), F32, (tm, width), lambda i: (i, 0))]
        return outs + extra

    extra_c = [((nbp, iw, lq), BF16, (None, iw, tm), lambda i: (i // tpb, 0, i % tpb)),
               ((nbp, lq, LANES), BF16, (None, tm, LANES), lambda i: (i // tpb, i % tpb, 0)),
               ((nbp, 8, lq), F32, (None, 8, tm), lambda i: (i // tpb, 0, i % tpb)),
               ((nbp, lq, LANES), F32, (None, tm, LANES), lambda i: (i // tpb, i % tpb, 0))]

    def sample_proj(w_bf, gn, gq, groups):
        outs = [((tm_s, w_bf.shape[1]), F32, (tm_s, w_bf.shape[1]), lambda i: (i, 0))]
        y = _proj(hs, gn, w_bf, cos_s, sin_s, gq, _row_sinks(groups), tm_s, outs)[0]
        return y, y[:rows_s].reshape(gs.nb, cfg.dec_seq, -1)

    def sample_rows(o, width):
        return jnp.pad(o[:, :cfg.dec_seq].reshape(rows_s, width), ((0, tm_s - rows_s), (0, 0)))

    rows_ab_p, rows_ab_s, rows_c_p, rows_c_s = [], [], [], []
    for l in range(cfg.depth):
        i = l // 2
        if l % 2 == 0:
            lam_init = 0.8 - 0.6 * float(np.exp(-0.3 * l))
            w = w_in_ab[i]
            wq = [w[:, j * aw:(j + 1) * aw] for j in range(4)] + [w[:, 4 * aw + j * bw:4 * aw + (j + 1) * bw] for j in range(4)]
            w_perm = jnp.concatenate([wq[0], wq[4], wq[1], wq[5], wq[2], wq[6], wq[3], wq[7]], axis=1).astype(BF16)
            gq = jnp.tile(g_qk_a[i], (1, 2))
            gq = jnp.pad(gq, ((0, 8 - gq.shape[0]), (0, 0)))
            w_out = w_out_ab[i].astype(BF16)
            gsub = g_sub_a[i][None, :]
            gn = g_norm_ab[i][None, :]
            a_shape, b_shape = (cfg.a_heads, 2 * HEAD_DIM), (cfg.b_heads, HEAD_DIM)

            qT, kbf, vT, kv, gates = _proj(hp, gn, w_perm, cos_p, sin_p, gq, with_sinks(groups_ab, sinks_ab), tm,
                                           prompt_outs(abw, []))
            new = kv[:, :t_real]
            rows_ab_p.append((new[..., 0:aw].reshape(nbp, t_real, *a_shape),
                              new[..., abw:abw + aw].reshape(nbp, t_real, *a_shape),
                              new[..., aw:abw].reshape(nbp, t_real, *b_shape),
                              new[..., abw + aw:2 * abw].reshape(nbp, t_real, *b_shape)))
            oa = _attn_a(gp, lam_a[i], qT, kbf, vT, cfg.a_heads, lam_init)
            ob = _attn_b(gp, qT, kbf, vT, cfg.b_heads // 2, cfg.a_heads)
            hp = _out_ab(oa.reshape(-1, aw), ob.reshape(-1, bw), gates, 0, hp, gsub, w_out, cfg.a_heads,
                         1.0 - lam_init, tm)

            y, new = sample_proj(w_perm, gn, gq, groups_ab)
            nbs = gs.nb
            rows_ab_s.append((new[..., abw:abw + aw].reshape(nbs, cfg.dec_seq, *a_shape),
                              new[..., 2 * abw:2 * abw + aw].reshape(nbs, cfg.dec_seq, *a_shape),
                              new[..., abw + aw:2 * abw].reshape(nbs, cfg.dec_seq, *b_shape),
                              new[..., 2 * abw + aw:3 * abw].reshape(nbs, cfg.dec_seq, *b_shape)))
            past_k = jnp.concatenate([cache_a_k[i].reshape(nbs, cfg.past_len, aw),
                                      cache_b_k[i].reshape(nbs, cfg.past_len, bw)], axis=-1)
            past_v = jnp.concatenate([cache_a_v[i].reshape(nbs, cfg.past_len, aw),
                                      cache_b_v[i].reshape(nbs, cfg.past_len, bw)], axis=-1)
            qT = _to_qT(new[..., 0:abw], gs)
            kbf = _to_k(jnp.concatenate([past_k, new[..., abw:2 * abw]], axis=1), gs)
            vT = _to_vT(jnp.concatenate([past_v, new[..., 2 * abw:3 * abw]], axis=1), gs)
            oa = _attn_a(gs, lam_a[i], qT, kbf, vT, cfg.a_heads, lam_init)
            ob = _attn_b(gs, qT, kbf, vT, cfg.b_heads // 2, cfg.a_heads)
            hs = _out_ab(sample_rows(oa, aw), sample_rows(ob, bw), y, 3, hs, gsub, w_out, cfg.a_heads,
                         1.0 - lam_init, tm_s)
        else:
            w = w_in_c[i]
            w_pad = jnp.pad(w, ((0, 0), (0, nc_pad - w.shape[1]))).astype(BF16)
            gq = jnp.tile(g_qk_c[i], (1, 2))
            gq = jnp.pad(gq, ((0, 8 - gq.shape[0]), (0, 0)))
            w_out = w_out_c[i].astype(BF16)
            gn = g_norm_c[i][None, :]
            c_shape = (cfg.c_heads, HEAD_DIM)
            ki_col = 4 * cw + iw

            qT, kbf, vT, kv, gates, qiT, ki_bf, wiT, ki_f = _proj(hp, gn, w_pad, cos_p, sin_p, gq,
                                                                 with_sinks(groups_c, sinks_c), tm, prompt_outs(cw, extra_c))
            new = kv[:, :t_real]
            rows_c_p.append((new[..., 0:cw].reshape(nbp, t_real, *c_shape),
                             new[..., cw:2 * cw].reshape(nbp, t_real, *c_shape),
                             ki_f[:, :t_real, :HEAD_DIM]))
            oc = _dsa(gp, qiT, wiT, ki_bf, qT, kbf, vT, n_sel_p, cfg.idx_heads, cfg.c_heads // 2)
            hp = _out_c(oc.reshape(-1, cw), gates, 0, hp, w_out, tm)

            y, new = sample_proj(w_pad, gn, gq, groups_c)
            nbs = gs.nb
            ki_new = new[..., ki_col:ki_col + LANES]
            rows_c_s.append((new[..., cw:2 * cw].reshape(nbs, cfg.dec_seq, *c_shape),
                             new[..., 2 * cw:3 * cw].reshape(nbs, cfg.dec_seq, *c_shape),
                             ki_new[..., :HEAD_DIM]))
            k_all = jnp.concatenate([cache_c_k[i].reshape(nbs, cfg.past_len, cw), new[..., cw:2 * cw]], axis=1)
            v_all = jnp.concatenate([cache_c_v[i].reshape(nbs, cfg.past_len, cw), new[..., 2 * cw:3 * cw]], axis=1)
            ki_past = jnp.pad(cache_c_kidx[i], ((0, 0), (0, 0), (0, LANES - HEAD_DIM)))
            ki_all = jnp.concatenate([ki_past, jnp.where(jnp.arange(LANES) < HEAD_DIM, ki_new, 0.0)], axis=1)
            wi_new = ki_new[..., HEAD_DIM:HEAD_DIM + 8]
            wiT = jnp.pad(jnp.swapaxes(wi_new, 1, 2), ((0, 0), (0, 0), (0, gs.lq - cfg.dec_seq)))
            oc = _dsa(gs, _to_qT(new[..., 4 * cw:ki_col], gs), wiT, _to_k(ki_all, gs), _to_qT(new[..., 0:cw], gs),
                      _to_k(k_all, gs), _to_vT(v_all, gs), n_sel_s, cfg.idx_heads, cfg.c_heads // 2)
            hs = _out_c(sample_rows(oc, cw), y, 3, hs, w_out, tm_s)

    y_prompt = hp.reshape(cfg.batch, lq, d)[:, cfg.n_meta:t_real]
    y_sample = hs[:rows_s].reshape(cfg.dec_batch, cfg.dec_seq, d)

    def stack(rows, j):
        return jnp.stack([r[j] for r in rows], axis=0)

    return (y_prompt, y_sample,
            stack(rows_ab_p, 0), stack(rows_ab_p, 1), stack(rows_ab_p, 2), stack(rows_ab_p, 3),
            stack(rows_c_p, 0), stack(rows_c_p, 1), stack(rows_c_p, 2),
            stack(rows_ab_s, 0), stack(rows_ab_s, 1), stack(rows_ab_s, 2), stack(rows_ab_s, 3),
            stack(rows_c_s, 0), stack(rows_c_s, 1), stack(rows_c_s, 2))


def kernel(x_prompt, x_sample, cache_a_k, cache_a_v, cache_b_k, cache_b_v, cache_c_k, cache_c_v, cache_c_kidx,
           meta_tokens, g_norm_ab, w_in_ab, g_qk_a, lam_a, g_sub_a, w_out_ab, g_norm_c, w_in_c, g_qk_c, w_out_c):
    return _forward(Cfg(), x_prompt, x_sample, cache_a_k, cache_a_v, cache_b_k, cache_b_v, cache_c_k, cache_c_v,
                    cache_c_kidx, meta_tokens, g_norm_ab, w_in_ab, g_qk_a, lam_a, g_sub_a, w_out_ab,
                    g_norm_c, w_in_c, g_qk_c, w_out_c)
```

```python
import functools
from typing import NamedTuple

import numpy as np
import jax
import jax.numpy as jnp
from jax import lax
from jax.experimental import pallas as pl
from jax.experimental.pallas import tpu as pltpu

F32 = jnp.float32
BF16 = jnp.bfloat16
I32 = jnp.int32
I16 = jnp.int16

LANES = 128
TILE = 256
VMEM_LIMIT = 56 * 1024 * 1024

ROPE_THETA = 10000.0
EPS = 1e-6
NEG = -1e30
PAD_CHUNK = 2 ** 30
HEAD_DIM = 64
MIN16 = -2 ** 15
LOG2E = 1.4426950408889634
ONES_ROWS = 16
EXP2_ZERO = -151.0


class Cfg(NamedTuple):
    d_model: int = 1024
    batch: int = 2
    seq: int = 8192
    depth: int = 4
    dec_batch: int = 8
    dec_seq: int = 16
    past_len: int = 2048
    chunk: int = 64
    n_meta: int = 16
    topk_max: int = 256
    a_heads: int = 4
    b_heads: int = 8
    c_heads: int = 16
    idx_heads: int = 4


def _round_up(x, m):
    return -(-x // m) * m


def _log2(n):
    l = int(n).bit_length() - 1
    assert (1 << l) == n, n
    return l


class Geom:
    def __init__(self, cfg, kind):
        self.kind = kind
        self.cfg = cfg
        self.sh = _log2(cfg.chunk)
        if kind == "prompt":
            self.nb = cfg.batch
            self.t_real = cfg.n_meta + cfg.seq
            self.lq = _round_up(self.t_real, 2 * TILE)
            self.lk = self.lq
            self.tq = TILE
            self.tq_sel = 2 * TILE
            self.n_real_q = self.t_real
        else:
            self.nb = cfg.dec_batch
            self.n_keys = cfg.past_len + cfg.dec_seq
            self.lq = LANES
            self.lk = _round_up(self.n_keys, 2 * TILE)
            self.tq = LANES
            self.tq_sel = LANES
            self.n_real_q = cfg.dec_seq
        self.tk = TILE
        self.nq = self.lq // self.tq
        self.nk = self.lk // self.tk

    def qpos(self, i):
        return i if self.kind == "prompt" else i + self.cfg.past_len

    def _cid_prompt(self, p):
        c = self.cfg
        body = lax.shift_right_logical(jnp.maximum(p - c.n_meta, 0), self.sh) + 1
        return jnp.where(p >= self.t_real, PAD_CHUNK, jnp.where(p < c.n_meta, 0, body))

    def qcid(self, pos):
        if self.kind == "prompt":
            return self._cid_prompt(pos)
        return lax.shift_right_logical(pos, self.sh)

    def kcid(self, j):
        if self.kind == "prompt":
            return self._cid_prompt(j)
        return jnp.where(j >= self.n_keys, PAD_CHUNK, lax.shift_right_logical(j, self.sh))

    def kend_cid(self, qpos):
        c = self.cfg
        if self.kind == "prompt":
            end = jnp.minimum(c.n_meta + c.chunk * self._cid_prompt(jnp.minimum(qpos, self.t_real - 1)),
                              self.t_real)
            return jnp.where(qpos >= self.t_real, self.lk, end)
        return jnp.minimum(c.chunk * (lax.shift_right_logical(qpos, self.sh) + 1), self.n_keys)

    def kend_pos(self, qpos):
        return jnp.minimum(qpos, self.lk)

    def tiles(self, q0, rule, tq=None):
        kend = self.kend_cid if rule == "cid" else self.kend_pos
        first = kend(self.qpos(q0))
        last = kend(self.qpos(jnp.minimum(q0 + (tq or self.tq) - 1, self.n_real_q - 1)))
        n_vis = (last + self.tk - 1) // self.tk
        n_full = first // self.tk
        return jnp.where(self.is_real(q0), n_vis, 0), jnp.where(self.is_real(q0), n_full, 0)

    def is_real(self, q0):
        return q0 < self.n_real_q


def _proj_kernel(x_ref, g_ref, w_ref, cos_ref, sin_ref, gq_ref, *o_refs, groups):
    x = x_ref[...]
    ms = jnp.mean(x * x, axis=-1, keepdims=True)
    h = (x * lax.rsqrt(ms + EPS) * g_ref[...]).astype(BF16)
    cos = cos_ref[...]
    sin_s = sin_ref[...]
    lane = lax.broadcasted_iota(I32, (1, LANES), 1)
    first_half = (lane & (HEAD_DIM - 1)) < (HEAD_DIM // 2)
    r = lax.broadcasted_iota(I32, (LANES, LANES), 0)
    c = lax.broadcasted_iota(I32, (LANES, LANES), 1)
    bd = jnp.where(lax.shift_right_logical(r, 6) == lax.shift_right_logical(c, 6), 1.0, 0.0).astype(BF16)

    def rope(y):
        part = jnp.where(first_half, pltpu.roll(y, LANES - HEAD_DIM // 2, 1), pltpu.roll(y, HEAD_DIM // 2, 1))
        return y * cos + part * sin_s

    def emit(out, j, sinks):
        for (oi, mode, off) in sinks:
            o_ref = o_refs[oi]
            lo_, hi_ = off + j * LANES, off + (j + 1) * LANES
            if mode == "row":
                o_ref[:, lo_:hi_] = out.astype(o_ref.dtype)
            elif mode == "T":
                o_ref[lo_:hi_, :] = out.T.astype(o_ref.dtype)
            elif mode == "row_lo":
                o_ref[:, lo_:hi_] = jnp.where(lane < HEAD_DIM, out, 0.0).astype(o_ref.dtype)
            else:
                o_ref[...] = out.T[HEAD_DIM:HEAD_DIM + 8, :].astype(o_ref.dtype)

    for (c0, c1, kind, gi, scale, sinks) in groups:
        y = jnp.dot(h, w_ref[:, c0:c1], preferred_element_type=F32)
        for j in range((c1 - c0) // LANES):
            yb = y[:, j * LANES:(j + 1) * LANES]
            if kind == "plain":
                out = yb
            elif kind == "norm_rope":
                sq = yb * yb
                hi = sq.astype(BF16)
                lo = (sq - hi.astype(F32)).astype(BF16)
                ss = jnp.dot(hi, bd, preferred_element_type=F32) + jnp.dot(lo, bd, preferred_element_type=F32)
                yb = yb * lax.rsqrt(ss * (1.0 / HEAD_DIM) + EPS) * gq_ref[gi:gi + 1, :]
                out = rope(yb)
            elif kind == "rope":
                out = rope(yb)
            else:
                out = jnp.where(lane < HEAD_DIM, rope(yb), jnp.where(lane < HEAD_DIM + 4, yb, 0.0))
            if scale != 1.0:
                out = out * jnp.where(lane < HEAD_DIM, 1.0, scale) if kind == "kiwi" else out * scale
            emit(out, j, sinks)


def _proj(x2d, g, w_bf, cos_t, sin_t, gq, groups, tm, outs):
    rows, d = x2d.shape
    n = w_bf.shape[1]
    period = cos_t.shape[0] // tm
    res = pl.pallas_call(
        functools.partial(_proj_kernel, groups=groups),
        grid=(rows // tm,),
        in_specs=[
            pl.BlockSpec((tm, d), lambda i: (i, 0)),
            pl.BlockSpec((1, d), lambda i: (0, 0)),
            pl.BlockSpec((d, n), lambda i: (0, 0)),
            pl.BlockSpec((tm, LANES), lambda i: (i % period, 0)),
            pl.BlockSpec((tm, LANES), lambda i: (i % period, 0)),
            pl.BlockSpec(gq.shape, lambda i: (0, 0)),
        ],
        out_specs=[pl.BlockSpec(blk, imap) for (_, _, blk, imap) in outs],
        out_shape=[jax.ShapeDtypeStruct(shape, dt) for (shape, dt, _, _) in outs],
        compiler_params=pltpu.CompilerParams(dimension_semantics=("arbitrary",), vmem_limit_bytes=VMEM_LIMIT),
        name="proj",
    )(x2d, g, w_bf, cos_t, sin_t, gq)
    return res


def _row_sinks(groups):
    return tuple((c0, c1, kind, gi, scale, ((0, "row", c0),)) for (c0, c1, kind, gi, scale) in groups)


def _half_rows(qT):
    qf = qT.astype(F32)
    row = lax.broadcasted_iota(I32, qf.shape, 0)
    return (jnp.where(row < HEAD_DIM, qf, 0.0).astype(BF16), jnp.where(row >= HEAD_DIM, qf, 0.0).astype(BF16))


def _pipelined_softmax(nk, n_full, n_vis, qk, softmax, pv):
    def step(kt, slot, masked):
        qk(jnp.minimum(kt + 1, nk - 1), 1 - slot)
        pv(jnp.maximum(kt - 1, 0), 1 - slot)
        softmax(kt, slot, masked)

    def pair(masked):
        def body(j, carry):
            step(2 * j, 0, masked)
            step(2 * j + 1, 1, masked)
            return carry
        return body

    n_pairs = (n_vis + 1) // 2
    n_full_pairs = n_full // 2
    qk(0, 0)
    lax.fori_loop(0, n_full_pairs, pair(False), 0)
    lax.fori_loop(n_full_pairs, n_pairs, pair(True), 0)
    pv(jnp.maximum(2 * n_pairs - 1, 0), 1)


def _softmax_tile(s, m_ref, al_ref, p_ref, slot, idx):
    m_old = m_ref[idx]
    m_new = jnp.maximum(m_old, jnp.max(s, axis=0, keepdims=True))
    al_ref[idx] = jnp.exp2(m_old - m_new)
    p_ref[slot, idx] = jnp.exp2(s - m_new).astype(BF16)
    m_ref[idx] = m_new


def _init_pipeline(m_s, acc_s, p_buf, al_s):
    m_s[...] = jnp.full(m_s.shape, NEG, F32)
    acc_s[...] = jnp.zeros(acc_s.shape, F32)
    p_buf[1] = jnp.zeros(p_buf.shape[1:], BF16)
    al_s[...] = jnp.ones(al_s.shape, F32)


def _tile_rows(kt, tk):
    return pl.ds(pl.multiple_of(kt * tk, tk), tk)


def _attn_a_kernel(lam_ref, q_ref, k_ref, v_ref, o_ref, m_s, acc_s, s_buf, p_buf, al_s, *, geom, lam_init):
    tq, tk = geom.tq, geom.tk
    dv = 2 * HEAD_DIM
    q0 = pl.program_id(2) * tq
    qm = _half_rows(q_ref[...])
    kend = geom.kend_cid(geom.qpos(q0 + lax.broadcasted_iota(I32, (1, tq), 1)))
    key_row = lax.broadcasted_iota(I32, (tk, 1), 0)
    n_vis, n_full = geom.tiles(q0, "cid")
    _init_pipeline(m_s, acc_s, p_buf, al_s)
    ones = jnp.ones((ONES_ROWS, tk), BF16)

    def qk(kt, slot):
        k = k_ref[_tile_rows(kt, tk), :]
        for mp in range(2):
            s_buf[slot, mp] = jnp.dot(k, qm[mp], preferred_element_type=F32)

    def pv(kt, slot):
        va = jnp.concatenate([v_ref[kt], ones], axis=0)
        for mp in range(2):
            acc_s[mp] = al_s[mp] * acc_s[mp] + jnp.dot(va, p_buf[slot, mp], preferred_element_type=F32)

    def softmax(kt, slot, masked):
        if masked:
            vis = key_row < (kend - kt * tk)
        for mp in range(2):
            s = s_buf[slot, mp]
            if masked:
                s = jnp.where(vis, s, NEG)
            _softmax_tile(s, m_s, al_s, p_buf, slot, mp)

    _pipelined_softmax(geom.nk, n_full, n_vis, qk, softmax, pv)
    lp = lam_ref[...]
    lam = (jnp.exp(jnp.sum(lp[0:1] * lp[1:2], axis=-1, keepdims=True))
           - jnp.exp(jnp.sum(lp[2:3] * lp[3:4], axis=-1, keepdims=True)) + lam_init)
    oT = acc_s[0, :dv] / acc_s[0, dv:dv + 1] - lam * (acc_s[1, :dv] / acc_s[1, dv:dv + 1])
    o_ref[...] = jnp.where(geom.is_real(q0), oT, 0.0).T


def _attn_a(geom, lam_p, qT, kbf, vT, n_heads, lam_init):
    nb, tq, tk, nk = geom.nb, geom.tq, geom.tk, geom.nk
    return pl.pallas_call(
        functools.partial(_attn_a_kernel, geom=geom, lam_init=lam_init),
        grid=(nb, n_heads, geom.nq),
        in_specs=[
            pl.BlockSpec(lam_p.shape, lambda b, h, i: (0, 0)),
            pl.BlockSpec((None, LANES, tq), lambda b, h, i: (b, h, i)),
            pl.BlockSpec((None, geom.lk, LANES), lambda b, h, i: (b, 0, h)),
            pl.BlockSpec((None, nk, LANES, tk), lambda b, h, i: (b, 0, h, 0)),
        ],
        out_specs=pl.BlockSpec((None, tq, LANES), lambda b, h, i: (b, i, h)),
        out_shape=jax.ShapeDtypeStruct((nb, geom.lq, n_heads * LANES), F32),
        scratch_shapes=[pltpu.VMEM((2, 1, tq), F32), pltpu.VMEM((2, LANES + ONES_ROWS, tq), F32),
                        pltpu.VMEM((2, 2, tk, tq), F32), pltpu.VMEM((2, 2, tk, tq), BF16), pltpu.VMEM((2, 1, tq), F32)],
        compiler_params=pltpu.CompilerParams(dimension_semantics=("arbitrary",) * 3, vmem_limit_bytes=VMEM_LIMIT),
        name="attn_a",
    )(lam_p, qT, kbf, vT)


def _attn_b_kernel(q_ref, k_ref, v_ref, o_ref, c_s, acc_s, z_buf, hi_buf, lo_buf, lb_buf, a_buf, *, geom, n_heads):
    tq, tk = geom.tq, geom.tk
    q0 = pl.program_id(1) * tq
    qh = []
    for p in range(n_heads // 2):
        qh.extend(_half_rows(q_ref[p * LANES:(p + 1) * LANES, :]))
    qpos = geom.qpos(q0 + lax.broadcasted_iota(I32, (1, tq), 1))
    key_row = lax.broadcasted_iota(I32, (tk, 1), 0)
    n_vis, n_full = geom.tiles(q0, "pos")
    c_s[...] = jnp.zeros(c_s.shape, F32)
    acc_s[...] = jnp.zeros(acc_s.shape, F32)
    r = lax.broadcasted_iota(I32, (tk, tk), 0)
    c = lax.broadcasted_iota(I32, (tk, tk), 1)
    upper = jnp.where(c > r, 1.0, 0.0).astype(BF16)

    def body(kt, masked):
        k = k_ref[_tile_rows(kt, tk), :]
        v = v_ref[kt]
        if masked:
            vis = key_row < (qpos - kt * tk)
        for h in range(n_heads):
            p = h // 2
            z_buf[h] = jnp.dot(k[:, p * LANES:(p + 1) * LANES], qh[h], preferred_element_type=F32)
        sums = []
        for h in range(n_heads):
            z = z_buf[h]
            sp = jnp.maximum(z, 0.0) + jnp.log2(1.0 + jnp.exp2(-jnp.abs(z)))
            log_keep = -sp
            if masked:
                log_keep = jnp.where(vis, log_keep, 0.0)
            hi = log_keep.astype(BF16)
            hi_buf[h] = hi
            lo_buf[h] = (log_keep - hi.astype(F32)).astype(BF16)
            lb_buf[h] = z - sp
            sums.append(jnp.sum(log_keep, axis=0, keepdims=True))
        for h in range(n_heads):
            z_buf[h] = (jnp.dot(upper, hi_buf[h], preferred_element_type=F32)
                        + jnp.dot(upper, lo_buf[h], preferred_element_type=F32))
        for h in range(n_heads):
            a = jnp.exp2(lb_buf[h] + z_buf[h] + c_s[h])
            if masked:
                a = jnp.where(vis, a, 0.0)
            a_buf[h] = a.astype(BF16)
        for h in range(n_heads):
            acc_s[h] = acc_s[h] + jnp.dot(v[h * HEAD_DIM:(h + 1) * HEAD_DIM, :], a_buf[h], preferred_element_type=F32)
            c_s[h] = c_s[h] + sums[h]

    def alive():
        return jnp.max(c_s[...]) > EXP2_ZERO

    lax.fori_loop(0, n_vis - n_full, lambda i, carry: (body(n_vis - 1 - i, True), carry)[1], 0)

    def full_tile(state):
        i, _ = state
        body(n_full - 1 - i, False)
        return i + 1, alive()

    lax.while_loop(lambda state: (state[0] < n_full) & state[1], full_tile, (jnp.int32(0), alive()))
    for p in range(n_heads // 2):
        o_ref[:, p * LANES:(p + 1) * LANES] = jnp.concatenate([acc_s[2 * p], acc_s[2 * p + 1]], axis=0).T


def _attn_b(geom, qT, kbf, vT, n_pairs, blk0):
    nb, tq, tk, nk = geom.nb, geom.tq, geom.tk, geom.nk
    w = n_pairs * LANES
    assert (blk0 * LANES) % w == 0
    wb = blk0 * LANES // w
    n_heads = 2 * n_pairs
    return pl.pallas_call(
        functools.partial(_attn_b_kernel, geom=geom, n_heads=n_heads),
        grid=(nb, geom.nq),
        in_specs=[
            pl.BlockSpec((None, w, tq), lambda b, i: (b, wb, i)),
            pl.BlockSpec((None, geom.lk, w), lambda b, i: (b, 0, wb)),
            pl.BlockSpec((None, nk, w, tk), lambda b, i: (b, 0, wb, 0)),
        ],
        out_specs=pl.BlockSpec((None, tq, w), lambda b, i: (b, i, 0)),
        out_shape=jax.ShapeDtypeStruct((nb, geom.lq, w), F32),
        scratch_shapes=[pltpu.VMEM((n_heads, 1, tq), F32), pltpu.VMEM((n_heads, HEAD_DIM, tq), F32),
                        pltpu.VMEM((n_heads, tk, tq), F32), pltpu.VMEM((n_heads, tk, tq), BF16),
                        pltpu.VMEM((n_heads, tk, tq), BF16), pltpu.VMEM((n_heads, tk, tq), F32),
                        pltpu.VMEM((n_heads, tk, tq), BF16)],
        compiler_params=pltpu.CompilerParams(dimension_semantics=("arbitrary",) * 2, vmem_limit_bytes=VMEM_LIMIT),
        name="attn_b",
    )(qT, kbf, vT)


def _select_bias(qi_ref, wi_ref, ki_ref, hi_s, lo_s, bias_s, *, geom, tq, n_sel, n_idx, q0, n_vis, n_fill):
    tk = geom.tk
    kend = geom.kend_cid(geom.qpos(q0 + lax.broadcasted_iota(I32, (1, tq), 1)))
    key_row = lax.broadcasted_iota(I32, (tk, 1), 0)
    zpad = jnp.zeros((LANES - HEAD_DIM, tq), BF16)
    qh = [jnp.concatenate([qi_ref[h * HEAD_DIM:(h + 1) * HEAD_DIM, :], zpad], axis=0) for h in range(n_idx)]
    wi = wi_ref[...]

    def vis_of(kt):
        return key_row < (kend - kt * tk)

    def rows(kt):
        return pl.ds(pl.multiple_of(kt * tk, tk), tk)

    def score_tile(kt, carry):
        ki = ki_ref[rows(kt), :]
        sc = jnp.zeros((tk, tq), F32)
        for h in range(n_idx):
            rel = jnp.maximum(jnp.dot(ki, qh[h], preferred_element_type=F32), 0.0)
            sc = sc + wi[h:h + 1, :] * rel
        sc = jnp.where(sc == 0.0, 0.0, sc)
        sc = jnp.where(vis_of(kt), sc, NEG)
        bits = lax.bitcast_convert_type(sc, I32)
        key = jnp.where(bits < 0, bits ^ jnp.int32(0x7FFFFFFF), bits)
        hi_s[rows(kt), :] = lax.shift_right_arithmetic(key, 16).astype(I16)
        lo_s[rows(kt), :] = ((key & 0xFFFF) + MIN16).astype(I16)
        return carry

    lax.fori_loop(0, n_vis, score_tile, 0)

    def pad_tile(kt, carry):
        hi_s[rows(kt), :] = jnp.full((tk, tq), MIN16, I16)
        lo_s[rows(kt), :] = jnp.full((tk, tq), MIN16, I16)
        return carry

    lax.fori_loop(n_vis, n_fill, pad_tile, 0)

    def rows2(j):
        return pl.ds(pl.multiple_of(j * 2 * tk, 2 * tk), 2 * tk)

    def count16(ref, pred_fn):
        def step(j, acc):
            hit = jnp.where(pred_fn(ref[rows2(j), :]), jnp.int16(1), jnp.int16(0))
            for t in range(2 * tk // 16):
                acc = acc + hit[t * 16:(t + 1) * 16, :]
            return acc
        acc = lax.fori_loop(0, n_fill // 2, step, jnp.zeros((16, tq), I16))
        return jnp.sum(acc.astype(I32), axis=0, keepdims=True)

    def search16(ref, target):
        def bit_step(i, prefix):
            cand = prefix | lax.shift_left(jnp.int32(1), 15 - i)
            cand_s = (cand + MIN16).astype(I16)
            cnt = count16(ref, lambda x: x >= cand_s)
            return jnp.where(cnt >= target, cand, prefix)
        return (lax.fori_loop(0, 16, bit_step, jnp.zeros((1, tq), I32)) + MIN16).astype(I16)

    hi_thr = search16(hi_s, n_sel)
    n_low = n_sel - count16(hi_s, lambda x: x > hi_thr)

    def bucket_tile(j, carry):
        lo_s[rows2(j), :] = jnp.where(hi_s[rows2(j), :] == hi_thr, lo_s[rows2(j), :], jnp.int16(MIN16))
        return carry

    lax.fori_loop(0, n_fill // 2, bucket_tile, 0)
    lo_thr = search16(lo_s, n_low)
    need = (n_low - count16(lo_s, lambda x: x > lo_thr)).astype(F32)

    r = lax.broadcasted_iota(I32, (tk, tk), 0)
    c = lax.broadcasted_iota(I32, (tk, tk), 1)
    lower = jnp.where(c < r, 1.0, 0.0).astype(BF16)

    def bias_tile(kt, run):
        hi = hi_s[rows(kt), :]
        lo = lo_s[rows(kt), :]
        same = hi == hi_thr
        gt = jnp.where((hi > hi_thr) | (same & (lo > lo_thr)), jnp.int16(1), jnp.int16(0)).astype(I32) != 0
        eq_i = jnp.where(same & (lo == lo_thr), jnp.int16(1), jnp.int16(0)).astype(I32)
        eq = eq_i.astype(F32)
        rank = jnp.dot(lower, eq.astype(BF16), preferred_element_type=F32) + run
        sel = (gt | ((eq_i != 0) & (rank < need))) & vis_of(kt)
        bias_s[rows(kt), :] = jnp.where(sel, 0.0, NEG).astype(bias_s.dtype)
        return run + jnp.sum(eq, axis=0, keepdims=True)

    lax.fori_loop(0, n_vis, bias_tile, jnp.zeros((1, tq), F32))

    def fill_tile(kt, carry):
        bias_s[rows(kt), :] = jnp.full((tk, tq), NEG, bias_s.dtype)
        return carry

    lax.fori_loop(n_vis, n_fill, fill_tile, 0)


def _dsa_kernel(qi_ref, wi_ref, ki_ref, q_ref, k_ref, v_ref, o_ref, hi_s, lo_s, bias_s, m_s, acc_s, s_buf, p_buf,
                al_s, *, geom, n_sel, n_idx):
    tq, tk, tq_sel = geom.tq, geom.tk, geom.tq_sel
    q0_sel = pl.program_id(1) * tq_sel
    n_vis_sel, _ = geom.tiles(q0_sel, "cid", tq_sel)

    @pl.when(pl.program_id(2) == 0)
    def _():
        _select_bias(qi_ref, wi_ref, ki_ref, hi_s, lo_s, bias_s, geom=geom, tq=tq_sel, n_sel=n_sel, n_idx=n_idx,
                     q0=q0_sel, n_vis=n_vis_sel, n_fill=2 * ((n_vis_sel + 1) // 2))

    ones = jnp.ones((ONES_ROWS, tk), BF16)
    for part in range(tq_sel // tq):
        cols = slice(part * tq, (part + 1) * tq)
        n_vis, _ = geom.tiles(q0_sel + part * tq, "cid")
        qh = _half_rows(q_ref[:, cols])
        _init_pipeline(m_s, acc_s, p_buf, al_s)

        def qk(kt, slot, qh=qh):
            k = k_ref[_tile_rows(kt, tk), :]
            for hh in range(2):
                s_buf[slot, hh] = jnp.dot(k, qh[hh], preferred_element_type=F32)

        def pv(kt, slot):
            v = v_ref[kt]
            for hh in range(2):
                va = jnp.concatenate([v[hh * HEAD_DIM:(hh + 1) * HEAD_DIM, :], ones], axis=0)
                acc_s[hh] = al_s[hh] * acc_s[hh] + jnp.dot(va, p_buf[slot, hh], preferred_element_type=F32)

        def softmax(kt, slot, masked, cols=cols):
            bias = bias_s[_tile_rows(kt, tk), cols].astype(F32)
            for hh in range(2):
                _softmax_tile(s_buf[slot, hh] + bias, m_s, al_s, p_buf, slot, hh)

        _pipelined_softmax(geom.nk, 0, n_vis, qk, softmax, pv)
        oT = jnp.concatenate([acc_s[0, :HEAD_DIM] / acc_s[0, HEAD_DIM:HEAD_DIM + 1],
                              acc_s[1, :HEAD_DIM] / acc_s[1, HEAD_DIM:HEAD_DIM + 1]], axis=0)
        o_ref[cols, :] = jnp.where(geom.is_real(q0_sel + part * tq), oT, 0.0).T


def _dsa(geom, qiT, wiT, ki_bf, qT, kbf, vT, n_sel, n_idx, n_pairs):
    nb, tq, tk, nk, tqs = geom.nb, geom.tq, geom.tk, geom.nk, geom.tq_sel
    return pl.pallas_call(
        functools.partial(_dsa_kernel, geom=geom, n_sel=n_sel, n_idx=n_idx),
        grid=(nb, geom.lq // tqs, n_pairs),
        in_specs=[
            pl.BlockSpec((None, n_idx * HEAD_DIM, tqs), lambda b, i, h: (b, 0, i)),
            pl.BlockSpec((None, 8, tqs), lambda b, i, h: (b, 0, i)),
            pl.BlockSpec((None, geom.lk, LANES), lambda b, i, h: (b, 0, 0)),
            pl.BlockSpec((None, LANES, tqs), lambda b, i, h: (b, h, i)),
            pl.BlockSpec((None, geom.lk, LANES), lambda b, i, h: (b, 0, h)),
            pl.BlockSpec((None, nk, LANES, tk), lambda b, i, h: (b, 0, h, 0)),
        ],
        out_specs=pl.BlockSpec((None, tqs, LANES), lambda b, i, h: (b, i, h)),
        out_shape=jax.ShapeDtypeStruct((nb, geom.lq, n_pairs * LANES), F32),
        scratch_shapes=[pltpu.VMEM((geom.lk, tqs), I16), pltpu.VMEM((geom.lk, tqs), I16), pltpu.VMEM((geom.lk, tqs), BF16),
                        pltpu.VMEM((2, 1, tq), F32), pltpu.VMEM((2, HEAD_DIM + ONES_ROWS, tq), F32),
                        pltpu.VMEM((2, 2, tk, tq), F32), pltpu.VMEM((2, 2, tk, tq), BF16), pltpu.VMEM((2, 1, tq), F32)],
        compiler_params=pltpu.CompilerParams(dimension_semantics=("arbitrary",) * 3, vmem_limit_bytes=VMEM_LIMIT),
        name="dsa",
    )(qiT, wiT, ki_bf, qT, kbf, vT)


def _silu(g):
    return g * (1.0 / (1.0 + jnp.exp(-g)))


def _out_ab_kernel(oa_ref, ob_ref, g_ref, x_ref, gsub_ref, w_ref, o_ref, *, n_heads, scale):
    wa = n_heads * LANES
    acc = x_ref[...]
    g = g_ref[...]
    for h in range(n_heads):
        blk = oa_ref[:, h * LANES:(h + 1) * LANES]
        ms = jnp.mean(blk * blk, axis=-1, keepdims=True)
        nrm = (blk * lax.rsqrt(ms + EPS) * gsub_ref[...]) * scale
        mixed = (nrm * _silu(g[:, h * LANES:(h + 1) * LANES])).astype(BF16)
        acc = acc + jnp.dot(mixed, w_ref[h * LANES:(h + 1) * LANES, :], preferred_element_type=F32)
    mixed_b = (ob_ref[...] * _silu(g[:, wa:])).astype(BF16)
    o_ref[...] = acc + jnp.dot(mixed_b, w_ref[wa:, :], preferred_element_type=F32)


def _out_ab(oa, ob, y, gcol, x2d, gsub, w_bf, n_heads, scale, tm):
    rows, d = x2d.shape
    wa, wb = oa.shape[1], ob.shape[1]
    return pl.pallas_call(
        functools.partial(_out_ab_kernel, n_heads=n_heads, scale=scale),
        grid=(rows // tm,),
        in_specs=[
            pl.BlockSpec((tm, wa), lambda i: (i, 0)),
            pl.BlockSpec((tm, wb), lambda i: (i, 0)),
            pl.BlockSpec((tm, wa + wb), lambda i: (i, gcol)),
            pl.BlockSpec((tm, d), lambda i: (i, 0)),
            pl.BlockSpec((1, LANES), lambda i: (0, 0)),
            pl.BlockSpec((wa + wb, d), lambda i: (0, 0)),
        ],
        out_specs=pl.BlockSpec((tm, d), lambda i: (i, 0)),
        out_shape=jax.ShapeDtypeStruct((rows, d), F32),
        compiler_params=pltpu.CompilerParams(dimension_semantics=("arbitrary",), vmem_limit_bytes=VMEM_LIMIT),
        name="out_ab",
    )(oa, ob, y, x2d, gsub, w_bf)


def _out_c_kernel(o_ref_in, g_ref, x_ref, w_ref, o_ref):
    mixed = (o_ref_in[...] * _silu(g_ref[...])).astype(BF16)
    o_ref[...] = x_ref[...] + jnp.dot(mixed, w_ref[...], preferred_element_type=F32)


def _out_c(o, y, gcol, x2d, w_bf, tm):
    rows, d = x2d.shape
    wc = o.shape[1]
    return pl.pallas_call(
        _out_c_kernel,
        grid=(rows // tm,),
        in_specs=[
            pl.BlockSpec((tm, wc), lambda i: (i, 0)),
            pl.BlockSpec((tm, wc), lambda i: (i, gcol)),
            pl.BlockSpec((tm, d), lambda i: (i, 0)),
            pl.BlockSpec((wc, d), lambda i: (0, 0)),
        ],
        out_specs=pl.BlockSpec((tm, d), lambda i: (i, 0)),
        out_shape=jax.ShapeDtypeStruct((rows, d), F32),
        compiler_params=pltpu.CompilerParams(dimension_semantics=("arbitrary",), vmem_limit_bytes=VMEM_LIMIT),
        name="out_c",
    )(o, y, x2d, w_bf)


def _rope_tables(pos):
    half = HEAD_DIM // 2
    inv_freq = ROPE_THETA ** (-jnp.arange(half, dtype=F32) / half)
    ang = pos.astype(F32)[:, None] * inv_freq[None, :]
    cos, sin = jnp.cos(ang), jnp.sin(ang)
    return jnp.tile(cos, (1, 4)), jnp.concatenate([-sin, sin, -sin, sin], axis=1)


def _to_qT(q, geom):
    qT = jnp.swapaxes(q.astype(BF16), 1, 2)
    return jnp.pad(qT, ((0, 0), (0, 0), (0, geom.lq - qT.shape[2])))


def _to_k(k, geom):
    k = k.astype(BF16)
    return jnp.pad(k, ((0, 0), (0, geom.lk - k.shape[1]), (0, 0)))


def _to_vT(v, geom):
    v = _to_k(v, geom)
    nb, _, w = v.shape
    return jnp.swapaxes(v.reshape(nb, geom.nk, geom.tk, w), 2, 3)


def _forward(cfg, x_prompt, x_sample, cache_a_k, cache_a_v, cache_b_k, cache_b_v, cache_c_k, cache_c_v,
             cache_c_kidx, meta_tokens, g_norm_ab, w_in_ab, g_qk_a, lam_a, g_sub_a, w_out_ab,
             g_norm_c, w_in_c, g_qk_c, w_out_c):
    d = cfg.d_model
    gp, gs = Geom(cfg, "prompt"), Geom(cfg, "sample")
    t_real = gp.t_real
    nbp, lq, tm = gp.nb, gp.lq, TILE
    tpb = lq // tm
    aw = cfg.a_heads * 2 * HEAD_DIM
    bw = cfg.b_heads * HEAD_DIM
    cw = cfg.c_heads * HEAD_DIM
    iw = cfg.idx_heads * HEAD_DIM
    abw = aw + bw
    n_sel_p = min(cfg.topk_max, cfg.seq // 4)
    n_sel_s = min(cfg.topk_max, (cfg.past_len + cfg.dec_seq) // 4)

    meta = jnp.broadcast_to(meta_tokens.astype(x_prompt.dtype)[None], (cfg.batch, cfg.n_meta, d))
    hp = jnp.concatenate([meta, x_prompt, jnp.zeros((cfg.batch, lq - t_real, d), x_prompt.dtype)], axis=1)
    hp = hp.reshape(cfg.batch * lq, d)
    rows_s = cfg.dec_batch * cfg.dec_seq
    tm_s = _round_up(rows_s, 8)
    hs = jnp.pad(x_sample.reshape(rows_s, d), ((0, tm_s - rows_s), (0, 0)))

    cos_p, sin_p = _rope_tables(jnp.arange(lq, dtype=I32))
    cos_s, sin_s = _rope_tables(cfg.past_len + jnp.arange(tm_s, dtype=I32) % cfg.dec_seq)

    s_qk = HEAD_DIM ** -0.5
    s_q2 = s_qk * LOG2E
    groups_ab = ((0, aw, "norm_rope", 0, s_q2), (aw, abw, "plain", 0, s_q2),
                 (abw, abw + aw, "norm_rope", 1, 1.0), (abw + aw, 2 * abw, "plain", 0, 1.0),
                 (2 * abw, 3 * abw, "plain", 0, 1.0), (3 * abw, 4 * abw, "plain", 0, 1.0))
    sinks_ab = (((0, "T", 0),), ((0, "T", aw),), ((1, "row", 0), (3, "row", 0)), ((1, "row", aw), (3, "row", aw)),
                ((2, "T", 0), (3, "row", abw)), ((4, "row", 0),))
    half = cw // 2
    groups_c = tuple([(j * half, (j + 1) * half, "norm_rope", 0, s_q2) for j in range(2)]
                     + [(cw + j * half, cw + (j + 1) * half, "norm_rope", 1, 1.0) for j in range(2)]
                     + [(2 * cw, 3 * cw, "plain", 0, 1.0), (3 * cw, 4 * cw, "plain", 0, 1.0),
                        (4 * cw, 4 * cw + iw, "rope", 0, s_qk),
                        (4 * cw + iw, 4 * cw + iw + LANES, "kiwi", 0, cfg.idx_heads ** -0.5)])
    sinks_c = tuple([((0, "T", j * half),) for j in range(2)]
                    + [((1, "row", j * half), (3, "row", j * half)) for j in range(2)]
                    + [((2, "T", 0), (3, "row", cw)), ((4, "row", 0),), ((5, "T", 0),),
                       ((6, "row_lo", 0), (7, "T_wi", 0), (8, "row", 0))])
    nc_pad = 4 * cw + iw + LANES

    def with_sinks(groups, sinks):
        return tuple(g + (s,) for g, s in zip(groups, sinks))

    def prompt_outs(width, extra):
        outs = [((nbp, width, lq), BF16, (None, width, tm), lambda i: (i // tpb, 0, i % tpb)),
                ((nbp, lq, width), BF16, (None, tm, width), lambda i: (i // tpb, i % tpb, 0)),
                ((nbp, gp.nk, width, tm), BF16, (None, None, width, tm), lambda i: (i // tpb, i % tpb, 0, 0)),
                ((nbp, lq, 2 * width), F32, (None, tm, 2 * width), lambda i: (i // tpb, i % tpb, 0)),
                ((nbp * lq, width), F32, (tm, width), lambda i: (i, 0))]
        return outs + extra

    extra_c = [((nbp, iw, lq), BF16, (None, iw, tm), lambda i: (i // tpb, 0, i % tpb)),
               ((nbp, lq, LANES), BF16, (None, tm, LANES), lambda i: (i // tpb, i % tpb, 0)),
               ((nbp, 8, lq), F32, (None, 8, tm), lambda i: (i // tpb, 0, i % tpb)),
               ((nbp, lq, LANES), F32, (None, tm, LANES), lambda i: (i // tpb, i % tpb, 0))]

    def sample_proj(w_bf, gn, gq, groups):
        outs = [((tm_s, w_bf.shape[1]), F32, (tm_s, w_bf.shape[1]), lambda i: (i, 0))]
        y = _proj(hs, gn, w_bf, cos_s, sin_s, gq, _row_sinks(groups), tm_s, outs)[0]
        return y, y[:rows_s].reshape(gs.nb, cfg.dec_seq, -1)

    def sample_rows(o, width):
        return jnp.pad(o[:, :cfg.dec_seq].reshape(rows_s, width), ((0, tm_s - rows_s), (0, 0)))

    rows_ab_p, rows_ab_s, rows_c_p, rows_c_s = [], [], [], []
    for l in range(cfg.depth):
        i = l // 2
        if l % 2 == 0:
            lam_init = 0.8 - 0.6 * float(np.exp(-0.3 * l))
            w = w_in_ab[i]
            wq = [w[:, j * aw:(j + 1) * aw] for j in range(4)] + [w[:, 4 * aw + j * bw:4 * aw + (j + 1) * bw] for j in range(4)]
            w_perm = jnp.concatenate([wq[0], wq[4], wq[1], wq[5], wq[2], wq[6], wq[3], wq[7]], axis=1).astype(BF16)
            gq = jnp.tile(g_qk_a[i], (1, 2))
            gq = jnp.pad(gq, ((0, 8 - gq.shape[0]), (0, 0)))
            w_out = w_out_ab[i].astype(BF16)
            gsub = g_sub_a[i][None, :]
            gn = g_norm_ab[i][None, :]
            a_shape, b_shape = (cfg.a_heads, 2 * HEAD_DIM), (cfg.b_heads, HEAD_DIM)

            qT, kbf, vT, kv, gates = _proj(hp, gn, w_perm, cos_p, sin_p, gq, with_sinks(groups_ab, sinks_ab), tm,
                                           prompt_outs(abw, []))
            new = kv[:, :t_real]
            rows_ab_p.append((new[..., 0:aw].reshape(nbp, t_real, *a_shape),
                              new[..., abw:abw + aw].reshape(nbp, t_real, *a_shape),
                              new[..., aw:abw].reshape(nbp, t_real, *b_shape),
                              new[..., abw + aw:2 * abw].reshape(nbp, t_real, *b_shape)))
            oa = _attn_a(gp, lam_a[i], qT, kbf, vT, cfg.a_heads, lam_init)
            ob = _attn_b(gp, qT, kbf, vT, cfg.b_heads // 2, cfg.a_heads)
            hp = _out_ab(oa.reshape(-1, aw), ob.reshape(-1, bw), gates, 0, hp, gsub, w_out, cfg.a_heads,
                         1.0 - lam_init, tm)

            y, new = sample_proj(w_perm, gn, gq, groups_ab)
            nbs = gs.nb
            rows_ab_s.append((new[..., abw:abw + aw].reshape(nbs, cfg.dec_seq, *a_shape),
                              new[..., 2 * abw:2 * abw + aw].reshape(nbs, cfg.dec_seq, *a_shape),
                              new[..., abw + aw:2 * abw].reshape(nbs, cfg.dec_seq, *b_shape),
                              new[..., 2 * abw + aw:3 * abw].reshape(nbs, cfg.dec_seq, *b_shape)))
            past_k = jnp.concatenate([cache_a_k[i].reshape(nbs, cfg.past_len, aw),
                                      cache_b_k[i].reshape(nbs, cfg.past_len, bw)], axis=-1)
            past_v = jnp.concatenate([cache_a_v[i].reshape(nbs, cfg.past_len, aw),
                                      cache_b_v[i].reshape(nbs, cfg.past_len, bw)], axis=-1)
            qT = _to_qT(new[..., 0:abw], gs)
            kbf = _to_k(jnp.concatenate([past_k, new[..., abw:2 * abw]], axis=1), gs)
            vT = _to_vT(jnp.concatenate([past_v, new[..., 2 * abw:3 * abw]], axis=1), gs)
            oa = _attn_a(gs, lam_a[i], qT, kbf, vT, cfg.a_heads, lam_init)
            ob = _attn_b(gs, qT, kbf, vT, cfg.b_heads // 2, cfg.a_heads)
            hs = _out_ab(sample_rows(oa, aw), sample_rows(ob, bw), y, 3, hs, gsub, w_out, cfg.a_heads,
                         1.0 - lam_init, tm_s)
        else:
            w = w_in_c[i]
            w_pad = jnp.pad(w, ((0, 0), (0, nc_pad - w.shape[1]))).astype(BF16)
            gq = jnp.tile(g_qk_c[i], (1, 2))
            gq = jnp.pad(gq, ((0, 8 - gq.shape[0]), (0, 0)))
            w_out = w_out_c[i].astype(BF16)
            gn = g_norm_c[i][None, :]
            c_shape = (cfg.c_heads, HEAD_DIM)
            ki_col = 4 * cw + iw

            qT, kbf, vT, kv, gates, qiT, ki_bf, wiT, ki_f = _proj(hp, gn, w_pad, cos_p, sin_p, gq,
                                                                 with_sinks(groups_c, sinks_c), tm, prompt_outs(cw, extra_c))
            new = kv[:, :t_real]
            rows_c_p.append((new[..., 0:cw].reshape(nbp, t_real, *c_shape),
                             new[..., cw:2 * cw].reshape(nbp, t_real, *c_shape),
                             ki_f[:, :t_real, :HEAD_DIM]))
            oc = _dsa(gp, qiT, wiT, ki_bf, qT, kbf, vT, n_sel_p, cfg.idx_heads, cfg.c_heads // 2)
            hp = _out_c(oc.reshape(-1, cw), gates, 0, hp, w_out, tm)

            y, new = sample_proj(w_pad, gn, gq, groups_c)
            nbs = gs.nb
            ki_new = new[..., ki_col:ki_col + LANES]
            rows_c_s.append((new[..., cw:2 * cw].reshape(nbs, cfg.dec_seq, *c_shape),
                             new[..., 2 * cw:3 * cw].reshape(nbs, cfg.dec_seq, *c_shape),
                             ki_new[..., :HEAD_DIM]))
            k_all = jnp.concatenate([cache_c_k[i].reshape(nbs, cfg.past_len, cw), new[..., cw:2 * cw]], axis=1)
            v_all = jnp.concatenate([cache_c_v[i].reshape(nbs, cfg.past_len, cw), new[..., 2 * cw:3 * cw]], axis=1)
            ki_past = jnp.pad(cache_c_kidx[i], ((0, 0), (0, 0), (0, LANES - HEAD_DIM)))
            ki_all = jnp.concatenate([ki_past, jnp.where(jnp.arange(LANES) < HEAD_DIM, ki_new, 0.0)], axis=1)
            wi_new = ki_new[..., HEAD_DIM:HEAD_DIM + 8]
            wiT = jnp.pad(jnp.swapaxes(wi_new, 1, 2), ((0, 0), (0, 0), (0, gs.lq - cfg.dec_seq)))
            oc = _dsa(gs, _to_qT(new[..., 4 * cw:ki_col], gs), wiT, _to_k(ki_all, gs), _to_qT(new[..., 0:cw], gs),
                      _to_k(k_all, gs), _to_vT(v_all, gs), n_sel_s, cfg.idx_heads, cfg.c_heads // 2)
            hs = _out_c(sample_rows(oc, cw), y, 3, hs, w_out, tm_s)

    y_prompt = hp.reshape(cfg.batch, lq, d)[:, cfg.n_meta:t_real]
    y_sample = hs[:rows_s].reshape(cfg.dec_batch, cfg.dec_seq, d)

    def stack(rows, j):
        return jnp.stack([r[j] for r in rows], axis=0)

    return (y_prompt, y_sample,
            stack(rows_ab_p, 0), stack(rows_ab_p, 1), stack(rows_ab_p, 2), stack(rows_ab_p, 3),
            stack(rows_c_p, 0), stack(rows_c_p, 1), stack(rows_c_p, 2),
            stack(rows_ab_s, 0), stack(rows_ab_s, 1), stack(rows_ab_s, 2), stack(rows_ab_s, 3),
            stack(rows_c_s, 0), stack(rows_c_s, 1), stack(rows_c_s, 2))


def kernel(x_prompt, x_sample, cache_a_k, cache_a_v, cache_b_k, cache_b_v, cache_c_k, cache_c_v, cache_c_kidx,
           meta_tokens, g_norm_ab, w_in_ab, g_qk_a, lam_a, g_sub_a, w_out_ab, g_norm_c, w_in_c, g_qk_c, w_out_c):
    return _forward(Cfg(), x_prompt, x_sample, cache_a_k, cache_a_v, cache_b_k, cache_b_v, cache_c_k, cache_c_v,
                    cache_c_kidx, meta_tokens, g_norm_ab, w_in_ab, g_qk_a, lam_a, g_sub_a, w_out_ab,
                    g_norm_c, w_in_c, g_qk_c, w_out_c)
```

```python
import functools
from typing import NamedTuple

import numpy as np
import jax
import jax.numpy as jnp
from jax import lax
from jax.experimental import pallas as pl
from jax.experimental.pallas import tpu as pltpu

F32 = jnp.float32
BF16 = jnp.bfloat16
I32 = jnp.int32
I16 = jnp.int16

LANES = 128
TILE = 256
VMEM_LIMIT = 56 * 1024 * 1024

ROPE_THETA = 10000.0
EPS = 1e-6
NEG = -1e30
PAD_CHUNK = 2 ** 30
HEAD_DIM = 64
MIN16 = -2 ** 15
LOG2E = 1.4426950408889634
ONES_ROWS = 16
EXP2_ZERO = -151.0


class Cfg(NamedTuple):
    d_model: int = 1024
    batch: int = 2
    seq: int = 8192
    depth: int = 4
    dec_batch: int = 8
    dec_seq: int = 16
    past_len: int = 2048
    chunk: int = 64
    n_meta: int = 16
    topk_max: int = 256
    a_heads: int = 4
    b_heads: int = 8
    c_heads: int = 16
    idx_heads: int = 4


def _round_up(x, m):
    return -(-x // m) * m


def _log2(n):
    l = int(n).bit_length() - 1
    assert (1 << l) == n, n
    return l


class Geom:
    def __init__(self, cfg, kind):
        self.kind = kind
        self.cfg = cfg
        self.sh = _log2(cfg.chunk)
        if kind == "prompt":
            self.nb = cfg.batch
            self.t_real = cfg.n_meta + cfg.seq
            self.lq = _round_up(self.t_real, 2 * TILE)
            self.lk = self.lq
            self.tq = TILE
            self.tq_sel = 2 * TILE
            self.n_real_q = self.t_real
            self.v_rows = False
        else:
            self.nb = cfg.dec_batch
            self.n_keys = cfg.past_len + cfg.dec_seq
            self.lq = LANES
            self.lk = _round_up(self.n_keys, 2 * TILE)
            self.tq = LANES
            self.tq_sel = LANES
            self.n_real_q = cfg.dec_seq
            self.v_rows = True
        self.tk = TILE
        self.nq = self.lq // self.tq
        self.nk = self.lk // self.tk

    def qpos(self, i):
        return i if self.kind == "prompt" else i + self.cfg.past_len

    def _cid_prompt(self, p):
        c = self.cfg
        body = lax.shift_right_logical(jnp.maximum(p - c.n_meta, 0), self.sh) + 1
        return jnp.where(p >= self.t_real, PAD_CHUNK, jnp.where(p < c.n_meta, 0, body))

    def qcid(self, pos):
        if self.kind == "prompt":
            return self._cid_prompt(pos)
        return lax.shift_right_logical(pos, self.sh)

    def kcid(self, j):
        if self.kind == "prompt":
            return self._cid_prompt(j)
        return jnp.where(j >= self.n_keys, PAD_CHUNK, lax.shift_right_logical(j, self.sh))

    def kend_cid(self, qpos):
        c = self.cfg
        if self.kind == "prompt":
            end = jnp.minimum(c.n_meta + c.chunk * self._cid_prompt(jnp.minimum(qpos, self.t_real - 1)),
                              self.t_real)
            return jnp.where(qpos >= self.t_real, self.lk, end)
        return jnp.minimum(c.chunk * (lax.shift_right_logical(qpos, self.sh) + 1), self.n_keys)

    def kend_pos(self, qpos):
        return jnp.minimum(qpos, self.lk)

    def tiles(self, q0, rule, tq=None):
        kend = self.kend_cid if rule == "cid" else self.kend_pos
        first = kend(self.qpos(q0))
        last = kend(self.qpos(jnp.minimum(q0 + (tq or self.tq) - 1, self.n_real_q - 1)))
        n_vis = (last + self.tk - 1) // self.tk
        n_full = first // self.tk
        return jnp.where(self.is_real(q0), n_vis, 0), jnp.where(self.is_real(q0), n_full, 0)

    def is_real(self, q0):
        return q0 < self.n_real_q


def _proj_kernel(x_ref, g_ref, w_ref, cos_ref, sin_ref, gq_ref, *o_refs, groups):
    x = x_ref[...]
    ms = jnp.mean(x * x, axis=-1, keepdims=True)
    h = (x * lax.rsqrt(ms + EPS) * g_ref[...]).astype(BF16)
    cos = cos_ref[...]
    sin_s = sin_ref[...]
    lane = lax.broadcasted_iota(I32, (1, LANES), 1)
    first_half = (lane & (HEAD_DIM - 1)) < (HEAD_DIM // 2)
    r = lax.broadcasted_iota(I32, (LANES, LANES), 0)
    c = lax.broadcasted_iota(I32, (LANES, LANES), 1)
    bd = jnp.where(lax.shift_right_logical(r, 6) == lax.shift_right_logical(c, 6), 1.0, 0.0).astype(BF16)

    def rope(y):
        part = jnp.where(first_half, pltpu.roll(y, LANES - HEAD_DIM // 2, 1), pltpu.roll(y, HEAD_DIM // 2, 1))
        return y * cos + part * sin_s

    def emit(out, j, sinks):
        for (oi, mode, off) in sinks:
            o_ref = o_refs[oi]
            lo_, hi_ = off + j * LANES, off + (j + 1) * LANES
            if mode == "row":
                o_ref[:, lo_:hi_] = out.astype(o_ref.dtype)
            elif mode == "T":
                o_ref[lo_:hi_, :] = out.T.astype(o_ref.dtype)
            elif mode == "row_lo":
                o_ref[:, lo_:hi_] = jnp.where(lane < HEAD_DIM, out, 0.0).astype(o_ref.dtype)
            else:
                o_ref[...] = out.T[HEAD_DIM:HEAD_DIM + 8, :].astype(o_ref.dtype)

    for (c0, c1, kind, gi, scale, sinks) in groups:
        y = jnp.dot(h, w_ref[:, c0:c1], preferred_element_type=F32)
        for j in range((c1 - c0) // LANES):
            yb = y[:, j * LANES:(j + 1) * LANES]
            if kind == "plain":
                out = yb
            elif kind == "norm_rope":
                sq = yb * yb
                hi = sq.astype(BF16)
                lo = (sq - hi.astype(F32)).astype(BF16)
                ss = jnp.dot(hi, bd, preferred_element_type=F32) + jnp.dot(lo, bd, preferred_element_type=F32)
                yb = yb * lax.rsqrt(ss * (1.0 / HEAD_DIM) + EPS) * gq_ref[gi:gi + 1, :]
                out = rope(yb)
            elif kind == "rope":
                out = rope(yb)
            else:
                out = jnp.where(lane < HEAD_DIM, rope(yb), jnp.where(lane < HEAD_DIM + 4, yb, 0.0))
            if scale != 1.0:
                out = out * jnp.where(lane < HEAD_DIM, 1.0, scale) if kind == "kiwi" else out * scale
            emit(out, j, sinks)


def _proj(x2d, g, w_bf, cos_t, sin_t, gq, groups, tm, outs):
    rows, d = x2d.shape
    n = w_bf.shape[1]
    period = cos_t.shape[0] // tm
    res = pl.pallas_call(
        functools.partial(_proj_kernel, groups=groups),
        grid=(rows // tm,),
        in_specs=[
            pl.BlockSpec((tm, d), lambda i: (i, 0)),
            pl.BlockSpec((1, d), lambda i: (0, 0)),
            pl.BlockSpec((d, n), lambda i: (0, 0)),
            pl.BlockSpec((tm, LANES), lambda i: (i % period, 0)),
            pl.BlockSpec((tm, LANES), lambda i: (i % period, 0)),
            pl.BlockSpec(gq.shape, lambda i: (0, 0)),
        ],
        out_specs=[pl.BlockSpec(blk, imap) for (_, _, blk, imap) in outs],
        out_shape=[jax.ShapeDtypeStruct(shape, dt) for (shape, dt, _, _) in outs],
        compiler_params=pltpu.CompilerParams(dimension_semantics=("arbitrary",), vmem_limit_bytes=VMEM_LIMIT),
        name="proj",
    )(x2d, g, w_bf, cos_t, sin_t, gq)
    return res


def _row_sinks(groups):
    return tuple((c0, c1, kind, gi, scale, ((0, "row", c0),)) for (c0, c1, kind, gi, scale) in groups)


def _half_rows(qT):
    qf = qT.astype(F32)
    row = lax.broadcasted_iota(I32, qf.shape, 0)
    return (jnp.where(row < HEAD_DIM, qf, 0.0).astype(BF16), jnp.where(row >= HEAD_DIM, qf, 0.0).astype(BF16))


def _pipelined_softmax(nk, n_full, n_vis, qk, softmax, pv):
    def step(kt, slot, masked):
        qk(jnp.minimum(kt + 1, nk - 1), 1 - slot)
        pv(jnp.maximum(kt - 1, 0), 1 - slot)
        softmax(kt, slot, masked)

    def pair(masked):
        def body(j, carry):
            step(2 * j, 0, masked)
            step(2 * j + 1, 1, masked)
            return carry
        return body

    n_pairs = (n_vis + 1) // 2
    n_full_pairs = n_full // 2
    qk(0, 0)
    lax.fori_loop(0, n_full_pairs, pair(False), 0)
    lax.fori_loop(n_full_pairs, n_pairs, pair(True), 0)
    pv(jnp.maximum(2 * n_pairs - 1, 0), 1)


def _softmax_tile(s, m_ref, al_ref, p_ref, slot, idx):
    m_old = m_ref[idx]
    m_new = jnp.maximum(m_old, jnp.max(s, axis=0, keepdims=True))
    al_ref[idx] = jnp.exp2(m_old - m_new)
    p_ref[slot, idx] = jnp.exp2(s - m_new).astype(BF16)
    m_ref[idx] = m_new


def _init_pipeline(m_s, acc_s, p_buf, al_s):
    m_s[...] = jnp.full(m_s.shape, NEG, F32)
    acc_s[...] = jnp.zeros(acc_s.shape, F32)
    p_buf[1] = jnp.zeros(p_buf.shape[1:], BF16)
    al_s[...] = jnp.ones(al_s.shape, F32)


def _tile_rows(kt, tk):
    return pl.ds(pl.multiple_of(kt * tk, tk), tk)


def _v_tile(v_ref, kt, geom):
    if geom.v_rows:
        return v_ref[_tile_rows(kt, geom.tk), :].T
    return v_ref[kt]


def _v_spec(geom, w, index_map):
    if geom.v_rows:
        return pl.BlockSpec((None, geom.lk, w), lambda *g: (index_map(*g)[0], 0, index_map(*g)[1]))
    return pl.BlockSpec((None, geom.nk, w, geom.tk), lambda *g: (index_map(*g)[0], 0, index_map(*g)[1], 0))


def _attn_a_kernel(lam_ref, q_ref, k_ref, v_ref, o_ref, m_s, acc_s, s_buf, p_buf, al_s, *, geom, lam_init):
    tq, tk = geom.tq, geom.tk
    dv = 2 * HEAD_DIM
    q0 = pl.program_id(2) * tq
    qm = _half_rows(q_ref[...])
    kend = geom.kend_cid(geom.qpos(q0 + lax.broadcasted_iota(I32, (1, tq), 1)))
    key_row = lax.broadcasted_iota(I32, (tk, 1), 0)
    n_vis, n_full = geom.tiles(q0, "cid")
    _init_pipeline(m_s, acc_s, p_buf, al_s)
    ones = jnp.ones((ONES_ROWS, tk), BF16)

    def qk(kt, slot):
        k = k_ref[_tile_rows(kt, tk), :]
        for mp in range(2):
            s_buf[slot, mp] = jnp.dot(k, qm[mp], preferred_element_type=F32)

    def pv(kt, slot):
        va = jnp.concatenate([_v_tile(v_ref, kt, geom), ones], axis=0)
        for mp in range(2):
            acc_s[mp] = al_s[mp] * acc_s[mp] + jnp.dot(va, p_buf[slot, mp], preferred_element_type=F32)

    def softmax(kt, slot, masked):
        if masked:
            vis = key_row < (kend - kt * tk)
        for mp in range(2):
            s = s_buf[slot, mp]
            if masked:
                s = jnp.where(vis, s, NEG)
            _softmax_tile(s, m_s, al_s, p_buf, slot, mp)

    _pipelined_softmax(geom.nk, n_full, n_vis, qk, softmax, pv)
    lp = lam_ref[...]
    lam = (jnp.exp(jnp.sum(lp[0:1] * lp[1:2], axis=-1, keepdims=True))
           - jnp.exp(jnp.sum(lp[2:3] * lp[3:4], axis=-1, keepdims=True)) + lam_init)
    oT = acc_s[0, :dv] / acc_s[0, dv:dv + 1] - lam * (acc_s[1, :dv] / acc_s[1, dv:dv + 1])
    o_ref[...] = jnp.where(geom.is_real(q0), oT, 0.0).T


def _attn_a(geom, lam_p, qT, kbf, vT, n_heads, lam_init):
    nb, tq, tk, nk = geom.nb, geom.tq, geom.tk, geom.nk
    return pl.pallas_call(
        functools.partial(_attn_a_kernel, geom=geom, lam_init=lam_init),
        grid=(nb, n_heads, geom.nq),
        in_specs=[
            pl.BlockSpec(lam_p.shape, lambda b, h, i: (0, 0)),
            pl.BlockSpec((None, LANES, tq), lambda b, h, i: (b, h, i)),
            pl.BlockSpec((None, geom.lk, LANES), lambda b, h, i: (b, 0, h)),
            _v_spec(geom, LANES, lambda b, h, i: (b, h)),
        ],
        out_specs=pl.BlockSpec((None, tq, LANES), lambda b, h, i: (b, i, h)),
        out_shape=jax.ShapeDtypeStruct((nb, geom.lq, n_heads * LANES), F32),
        scratch_shapes=[pltpu.VMEM((2, 1, tq), F32), pltpu.VMEM((2, LANES + ONES_ROWS, tq), F32),
                        pltpu.VMEM((2, 2, tk, tq), F32), pltpu.VMEM((2, 2, tk, tq), BF16), pltpu.VMEM((2, 1, tq), F32)],
        compiler_params=pltpu.CompilerParams(dimension_semantics=("arbitrary",) * 3, vmem_limit_bytes=VMEM_LIMIT),
        name="attn_a",
    )(lam_p, qT, kbf, vT)


def _attn_b_kernel(q_ref, k_ref, v_ref, o_ref, c_s, acc_s, z_buf, hi_buf, lo_buf, lb_buf, a_buf, *, geom, n_heads):
    tq, tk = geom.tq, geom.tk
    q0 = pl.program_id(1) * tq
    qh = []
    for p in range(n_heads // 2):
        qh.extend(_half_rows(q_ref[p * LANES:(p + 1) * LANES, :]))
    qpos = geom.qpos(q0 + lax.broadcasted_iota(I32, (1, tq), 1))
    key_row = lax.broadcasted_iota(I32, (tk, 1), 0)
    n_vis, n_full = geom.tiles(q0, "pos")
    c_s[...] = jnp.zeros(c_s.shape, F32)
    acc_s[...] = jnp.zeros(acc_s.shape, F32)
    r = lax.broadcasted_iota(I32, (tk, tk), 0)
    c = lax.broadcasted_iota(I32, (tk, tk), 1)
    upper = jnp.where(c > r, 1.0, 0.0).astype(BF16)

    def body(kt, masked):
        k = k_ref[_tile_rows(kt, tk), :]
        v = _v_tile(v_ref, kt, geom)
        if masked:
            vis = key_row < (qpos - kt * tk)
        for h in range(n_heads):
            p = h // 2
            z_buf[h] = jnp.dot(k[:, p * LANES:(p + 1) * LANES], qh[h], preferred_element_type=F32)
        sums = []
        for h in range(n_heads):
            z = z_buf[h]
            sp = jnp.maximum(z, 0.0) + jnp.log2(1.0 + jnp.exp2(-jnp.abs(z)))
            log_keep = -sp
            if masked:
                log_keep = jnp.where(vis, log_keep, 0.0)
            hi = log_keep.astype(BF16)
            hi_buf[h] = hi
            lo_buf[h] = (log_keep - hi.astype(F32)).astype(BF16)
            lb_buf[h] = z - sp
            sums.append(jnp.sum(log_keep, axis=0, keepdims=True))
        for h in range(n_heads):
            z_buf[h] = (jnp.dot(upper, hi_buf[h], preferred_element_type=F32)
                        + jnp.dot(upper, lo_buf[h], preferred_element_type=F32))
        for h in range(n_heads):
            a = jnp.exp2(lb_buf[h] + z_buf[h] + c_s[h])
            if masked:
                a = jnp.where(vis, a, 0.0)
            a_buf[h] = a.astype(BF16)
        for h in range(n_heads):
            acc_s[h] = acc_s[h] + jnp.dot(v[h * HEAD_DIM:(h + 1) * HEAD_DIM, :], a_buf[h], preferred_element_type=F32)
            c_s[h] = c_s[h] + sums[h]

    def alive():
        return jnp.max(c_s[...]) > EXP2_ZERO

    lax.fori_loop(0, n_vis - n_full, lambda i, carry: (body(n_vis - 1 - i, True), carry)[1], 0)

    def full_tile(state):
        i, _ = state
        body(n_full - 1 - i, False)
        return i + 1, alive()

    lax.while_loop(lambda state: (state[0] < n_full) & state[1], full_tile, (jnp.int32(0), alive()))
    for p in range(n_heads // 2):
        o_ref[:, p * LANES:(p + 1) * LANES] = jnp.concatenate([acc_s[2 * p], acc_s[2 * p + 1]], axis=0).T


def _attn_b(geom, qT, kbf, vT, n_pairs, blk0):
    nb, tq, tk, nk = geom.nb, geom.tq, geom.tk, geom.nk
    w = n_pairs * LANES
    assert (blk0 * LANES) % w == 0
    wb = blk0 * LANES // w
    n_heads = 2 * n_pairs
    return pl.pallas_call(
        functools.partial(_attn_b_kernel, geom=geom, n_heads=n_heads),
        grid=(nb, geom.nq),
        in_specs=[
            pl.BlockSpec((None, w, tq), lambda b, i: (b, wb, i)),
            pl.BlockSpec((None, geom.lk, w), lambda b, i: (b, 0, wb)),
            _v_spec(geom, w, lambda b, i: (b, wb)),
        ],
        out_specs=pl.BlockSpec((None, tq, w), lambda b, i: (b, i, 0)),
        out_shape=jax.ShapeDtypeStruct((nb, geom.lq, w), F32),
        scratch_shapes=[pltpu.VMEM((n_heads, 1, tq), F32), pltpu.VMEM((n_heads, HEAD_DIM, tq), F32),
                        pltpu.VMEM((n_heads, tk, tq), F32), pltpu.VMEM((n_heads, tk, tq), BF16),
                        pltpu.VMEM((n_heads, tk, tq), BF16), pltpu.VMEM((n_heads, tk, tq), F32),
                        pltpu.VMEM((n_heads, tk, tq), BF16)],
        compiler_params=pltpu.CompilerParams(dimension_semantics=("arbitrary",) * 2, vmem_limit_bytes=VMEM_LIMIT),
        name="attn_b",
    )(qT, kbf, vT)


def _select_bias(qi_ref, wi_ref, ki_ref, hi_s, lo_s, bias_s, *, geom, tq, n_sel, n_idx, q0, n_vis, n_fill):
    tk = geom.tk
    kend = geom.kend_cid(geom.qpos(q0 + lax.broadcasted_iota(I32, (1, tq), 1)))
    key_row = lax.broadcasted_iota(I32, (tk, 1), 0)
    zpad = jnp.zeros((LANES - HEAD_DIM, tq), BF16)
    qh = [jnp.concatenate([qi_ref[h * HEAD_DIM:(h + 1) * HEAD_DIM, :], zpad], axis=0) for h in range(n_idx)]
    wi = wi_ref[...]

    def vis_of(kt):
        return key_row < (kend - kt * tk)

    def rows(kt):
        return pl.ds(pl.multiple_of(kt * tk, tk), tk)

    def score_tile(kt, carry):
        ki = ki_ref[rows(kt), :]
        sc = jnp.zeros((tk, tq), F32)
        for h in range(n_idx):
            rel = jnp.maximum(jnp.dot(ki, qh[h], preferred_element_type=F32), 0.0)
            sc = sc + wi[h:h + 1, :] * rel
        sc = jnp.where(sc == 0.0, 0.0, sc)
        sc = jnp.where(vis_of(kt), sc, NEG)
        bits = lax.bitcast_convert_type(sc, I32)
        key = jnp.where(bits < 0, bits ^ jnp.int32(0x7FFFFFFF), bits)
        hi_s[rows(kt), :] = lax.shift_right_arithmetic(key, 16).astype(I16)
        lo_s[rows(kt), :] = ((key & 0xFFFF) + MIN16).astype(I16)
        return carry

    lax.fori_loop(0, n_vis, score_tile, 0)

    def pad_tile(kt, carry):
        hi_s[rows(kt), :] = jnp.full((tk, tq), MIN16, I16)
        lo_s[rows(kt), :] = jnp.full((tk, tq), MIN16, I16)
        return carry

    lax.fori_loop(n_vis, n_fill, pad_tile, 0)

    def rows2(j):
        return pl.ds(pl.multiple_of(j * 2 * tk, 2 * tk), 2 * tk)

    def count16(ref, pred_fn):
        def step(j, acc):
            hit = jnp.where(pred_fn(ref[rows2(j), :]), jnp.int16(1), jnp.int16(0))
            for t in range(2 * tk // 16):
                acc = acc + hit[t * 16:(t + 1) * 16, :]
            return acc
        acc = lax.fori_loop(0, n_fill // 2, step, jnp.zeros((16, tq), I16))
        return jnp.sum(acc.astype(I32), axis=0, keepdims=True)

    def search16(ref, target):
        def bit_step(i, prefix):
            cand = prefix | lax.shift_left(jnp.int32(1), 15 - i)
            cand_s = (cand + MIN16).astype(I16)
            cnt = count16(ref, lambda x: x >= cand_s)
            return jnp.where(cnt >= target, cand, prefix)
        return (lax.fori_loop(0, 16, bit_step, jnp.zeros((1, tq), I32)) + MIN16).astype(I16)

    hi_thr = search16(hi_s, n_sel)
    n_low = n_sel - count16(hi_s, lambda x: x > hi_thr)

    def bucket_tile(j, carry):
        lo_s[rows2(j), :] = jnp.where(hi_s[rows2(j), :] == hi_thr, lo_s[rows2(j), :], jnp.int16(MIN16))
        return carry

    lax.fori_loop(0, n_fill // 2, bucket_tile, 0)
    lo_thr = search16(lo_s, n_low)
    need = (n_low - count16(lo_s, lambda x: x > lo_thr)).astype(F32)

    r = lax.broadcasted_iota(I32, (tk, tk), 0)
    c = lax.broadcasted_iota(I32, (tk, tk), 1)
    lower = jnp.where(c < r, 1.0, 0.0).astype(BF16)

    def bias_tile(kt, run):
        hi = hi_s[rows(kt), :]
        lo = lo_s[rows(kt), :]
        same = hi == hi_thr
        gt = jnp.where((hi > hi_thr) | (same & (lo > lo_thr)), jnp.int16(1), jnp.int16(0)).astype(I32) != 0
        eq_i = jnp.where(same & (lo == lo_thr), jnp.int16(1), jnp.int16(0)).astype(I32)
        eq = eq_i.astype(F32)
        rank = jnp.dot(lower, eq.astype(BF16), preferred_element_type=F32) + run
        sel = (gt | ((eq_i != 0) & (rank < need))) & vis_of(kt)
        bias_s[rows(kt), :] = jnp.where(sel, 0.0, NEG).astype(bias_s.dtype)
        return run + jnp.sum(eq, axis=0, keepdims=True)

    lax.fori_loop(0, n_vis, bias_tile, jnp.zeros((1, tq), F32))

    def fill_tile(kt, carry):
        bias_s[rows(kt), :] = jnp.full((tk, tq), NEG, bias_s.dtype)
        return carry

    lax.fori_loop(n_vis, n_fill, fill_tile, 0)


def _dsa_kernel(qi_ref, wi_ref, ki_ref, q_ref, k_ref, v_ref, o_ref, hi_s, lo_s, bias_s, m_s, acc_s, s_buf, p_buf,
                al_s, *, geom, n_sel, n_idx):
    tq, tk, tq_sel = geom.tq, geom.tk, geom.tq_sel
    q0_sel = pl.program_id(1) * tq_sel
    n_vis_sel, _ = geom.tiles(q0_sel, "cid", tq_sel)

    @pl.when(pl.program_id(2) == 0)
    def _():
        _select_bias(qi_ref, wi_ref, ki_ref, hi_s, lo_s, bias_s, geom=geom, tq=tq_sel, n_sel=n_sel, n_idx=n_idx,
                     q0=q0_sel, n_vis=n_vis_sel, n_fill=2 * ((n_vis_sel + 1) // 2))

    ones = jnp.ones((ONES_ROWS, tk), BF16)
    for part in range(tq_sel // tq):
        cols = slice(part * tq, (part + 1) * tq)
        n_vis, _ = geom.tiles(q0_sel + part * tq, "cid")
        qh = _half_rows(q_ref[:, cols])
        _init_pipeline(m_s, acc_s, p_buf, al_s)

        def qk(kt, slot, qh=qh):
            k = k_ref[_tile_rows(kt, tk), :]
            for hh in range(2):
                s_buf[slot, hh] = jnp.dot(k, qh[hh], preferred_element_type=F32)

        def pv(kt, slot):
            v = _v_tile(v_ref, kt, geom)
            for hh in range(2):
                va = jnp.concatenate([v[hh * HEAD_DIM:(hh + 1) * HEAD_DIM, :], ones], axis=0)
                acc_s[hh] = al_s[hh] * acc_s[hh] + jnp.dot(va, p_buf[slot, hh], preferred_element_type=F32)

        def softmax(kt, slot, masked, cols=cols):
            bias = bias_s[_tile_rows(kt, tk), cols].astype(F32)
            for hh in range(2):
                _softmax_tile(s_buf[slot, hh] + bias, m_s, al_s, p_buf, slot, hh)

        _pipelined_softmax(geom.nk, 0, n_vis, qk, softmax, pv)
        oT = jnp.concatenate([acc_s[0, :HEAD_DIM] / acc_s[0, HEAD_DIM:HEAD_DIM + 1],
                              acc_s[1, :HEAD_DIM] / acc_s[1, HEAD_DIM:HEAD_DIM + 1]], axis=0)
        o_ref[cols, :] = jnp.where(geom.is_real(q0_sel + part * tq), oT, 0.0).T


def _dsa(geom, qiT, wiT, ki_bf, qT, kbf, vT, n_sel, n_idx, n_pairs):
    nb, tq, tk, nk, tqs = geom.nb, geom.tq, geom.tk, geom.nk, geom.tq_sel
    return pl.pallas_call(
        functools.partial(_dsa_kernel, geom=geom, n_sel=n_sel, n_idx=n_idx),
        grid=(nb, geom.lq // tqs, n_pairs),
        in_specs=[
            pl.BlockSpec((None, n_idx * HEAD_DIM, tqs), lambda b, i, h: (b, 0, i)),
            pl.BlockSpec((None, 8, tqs), lambda b, i, h: (b, 0, i)),
            pl.BlockSpec((None, geom.lk, LANES), lambda b, i, h: (b, 0, 0)),
            pl.BlockSpec((None, LANES, tqs), lambda b, i, h: (b, h, i)),
            pl.BlockSpec((None, geom.lk, LANES), lambda b, i, h: (b, 0, h)),
            _v_spec(geom, LANES, lambda b, i, h: (b, h)),
        ],
        out_specs=pl.BlockSpec((None, tqs, LANES), lambda b, i, h: (b, i, h)),
        out_shape=jax.ShapeDtypeStruct((nb, geom.lq, n_pairs * LANES), F32),
        scratch_shapes=[pltpu.VMEM((geom.lk, tqs), I16), pltpu.VMEM((geom.lk, tqs), I16), pltpu.VMEM((geom.lk, tqs), BF16),
                        pltpu.VMEM((2, 1, tq), F32), pltpu.VMEM((2, HEAD_DIM + ONES_ROWS, tq), F32),
                        pltpu.VMEM((2, 2, tk, tq), F32), pltpu.VMEM((2, 2, tk, tq), BF16), pltpu.VMEM((2, 1, tq), F32)],
        compiler_params=pltpu.CompilerParams(dimension_semantics=("arbitrary",) * 3, vmem_limit_bytes=VMEM_LIMIT),
        name="dsa",
    )(qiT, wiT, ki_bf, qT, kbf, vT)


def _silu(g):
    return g * (1.0 / (1.0 + jnp.exp(-g)))


def _out_ab_kernel(oa_ref, ob_ref, g_ref, x_ref, gsub_ref, w_ref, o_ref, *, n_heads, scale):
    wa = n_heads * LANES
    acc = x_ref[...]
    g = g_ref[...]
    for h in range(n_heads):
        blk = oa_ref[:, h * LANES:(h + 1) * LANES]
        ms = jnp.mean(blk * blk, axis=-1, keepdims=True)
        nrm = (blk * lax.rsqrt(ms + EPS) * gsub_ref[...]) * scale
        mixed = (nrm * _silu(g[:, h * LANES:(h + 1) * LANES])).astype(BF16)
        acc = acc + jnp.dot(mixed, w_ref[h * LANES:(h + 1) * LANES, :], preferred_element_type=F32)
    mixed_b = (ob_ref[...] * _silu(g[:, wa:])).astype(BF16)
    o_ref[...] = acc + jnp.dot(mixed_b, w_ref[wa:, :], preferred_element_type=F32)


def _out_ab(oa, ob, y, gcol, x2d, gsub, w_bf, n_heads, scale, tm):
    rows, d = x2d.shape
    wa, wb = oa.shape[1], ob.shape[1]
    return pl.pallas_call(
        functools.partial(_out_ab_kernel, n_heads=n_heads, scale=scale),
        grid=(rows // tm,),
        in_specs=[
            pl.BlockSpec((tm, wa), lambda i: (i, 0)),
            pl.BlockSpec((tm, wb), lambda i: (i, 0)),
            pl.BlockSpec((tm, wa + wb), lambda i: (i, gcol)),
            pl.BlockSpec((tm, d), lambda i: (i, 0)),
            pl.BlockSpec((1, LANES), lambda i: (0, 0)),
            pl.BlockSpec((wa + wb, d), lambda i: (0, 0)),
        ],
        out_specs=pl.BlockSpec((tm, d), lambda i: (i, 0)),
        out_shape=jax.ShapeDtypeStruct((rows, d), F32),
        compiler_params=pltpu.CompilerParams(dimension_semantics=("arbitrary",), vmem_limit_bytes=VMEM_LIMIT),
        name="out_ab",
    )(oa, ob, y, x2d, gsub, w_bf)


def _out_c_kernel(o_ref_in, g_ref, x_ref, w_ref, o_ref):
    mixed = (o_ref_in[...] * _silu(g_ref[...])).astype(BF16)
    o_ref[...] = x_ref[...] + jnp.dot(mixed, w_ref[...], preferred_element_type=F32)


def _out_c(o, y, gcol, x2d, w_bf, tm):
    rows, d = x2d.shape
    wc = o.shape[1]
    return pl.pallas_call(
        _out_c_kernel,
        grid=(rows // tm,),
        in_specs=[
            pl.BlockSpec((tm, wc), lambda i: (i, 0)),
            pl.BlockSpec((tm, wc), lambda i: (i, gcol)),
            pl.BlockSpec((tm, d), lambda i: (i, 0)),
            pl.BlockSpec((wc, d), lambda i: (0, 0)),
        ],
        out_specs=pl.BlockSpec((tm, d), lambda i: (i, 0)),
        out_shape=jax.ShapeDtypeStruct((rows, d), F32),
        compiler_params=pltpu.CompilerParams(dimension_semantics=("arbitrary",), vmem_limit_bytes=VMEM_LIMIT),
        name="out_c",
    )(o, y, x2d, w_bf)


def _rope_tables(pos):
    half = HEAD_DIM // 2
    inv_freq = ROPE_THETA ** (-jnp.arange(half, dtype=F32) / half)
    ang = pos.astype(F32)[:, None] * inv_freq[None, :]
    cos, sin = jnp.cos(ang), jnp.sin(ang)
    return jnp.tile(cos, (1, 4)), jnp.concatenate([-sin, sin, -sin, sin], axis=1)


def _to_qT(q, geom):
    qT = jnp.swapaxes(q.astype(BF16), 1, 2)
    return jnp.pad(qT, ((0, 0), (0, 0), (0, geom.lq - qT.shape[2])))


def _to_k(k, geom):
    k = k.astype(BF16)
    return jnp.pad(k, ((0, 0), (0, geom.lk - k.shape[1]), (0, 0)))


def _to_vT(v, geom):
    v = _to_k(v, geom)
    nb, _, w = v.shape
    return jnp.swapaxes(v.reshape(nb, geom.nk, geom.tk, w), 2, 3)


def _forward(cfg, x_prompt, x_sample, cache_a_k, cache_a_v, cache_b_k, cache_b_v, cache_c_k, cache_c_v,
             cache_c_kidx, meta_tokens, g_norm_ab, w_in_ab, g_qk_a, lam_a, g_sub_a, w_out_ab,
             g_norm_c, w_in_c, g_qk_c, w_out_c):
    d = cfg.d_model
    gp, gs = Geom(cfg, "prompt"), Geom(cfg, "sample")
    t_real = gp.t_real
    nbp, lq, tm = gp.nb, gp.lq, TILE
    tpb = lq // tm
    aw = cfg.a_heads * 2 * HEAD_DIM
    bw = cfg.b_heads * HEAD_DIM
    cw = cfg.c_heads * HEAD_DIM
    iw = cfg.idx_heads * HEAD_DIM
    abw = aw + bw
    n_sel_p = min(cfg.topk_max, cfg.seq // 4)
    n_sel_s = min(cfg.topk_max, (cfg.past_len + cfg.dec_seq) // 4)

    meta = jnp.broadcast_to(meta_tokens.astype(x_prompt.dtype)[None], (cfg.batch, cfg.n_meta, d))
    hp = jnp.concatenate([meta, x_prompt, jnp.zeros((cfg.batch, lq - t_real, d), x_prompt.dtype)], axis=1)
    hp = hp.reshape(cfg.batch * lq, d)
    rows_s = cfg.dec_batch * cfg.dec_seq
    tm_s = _round_up(rows_s, 8)
    hs = jnp.pad(x_sample.reshape(rows_s, d), ((0, tm_s - rows_s), (0, 0)))

    cos_p, sin_p = _rope_tables(jnp.arange(lq, dtype=I32))
    cos_s, sin_s = _rope_tables(cfg.past_len + jnp.arange(tm_s, dtype=I32) % cfg.dec_seq)

    s_qk = HEAD_DIM ** -0.5
    s_q2 = s_qk * LOG2E
    groups_ab = ((0, aw, "norm_rope", 0, s_q2), (aw, abw, "plain", 0, s_q2),
                 (abw, abw + aw, "norm_rope", 1, 1.0), (abw + aw, 2 * abw, "plain", 0, 1.0),
                 (2 * abw, 3 * abw, "plain", 0, 1.0), (3 * abw, 4 * abw, "plain", 0, 1.0))
    sinks_ab = (((0, "T", 0),), ((0, "T", aw),), ((1, "row", 0), (3, "row", 0)), ((1, "row", aw), (3, "row", aw)),
                ((2, "T", 0), (3, "row", abw)), ((4, "row", 0),))
    half = cw // 2
    groups_c = tuple([(j * half, (j + 1) * half, "norm_rope", 0, s_q2) for j in range(2)]
                     + [(cw + j * half, cw + (j + 1) * half, "norm_rope", 1, 1.0) for j in range(2)]
                     + [(2 * cw, 3 * cw, "plain", 0, 1.0), (3 * cw, 4 * cw, "plain", 0, 1.0),
                        (4 * cw, 4 * cw + iw, "rope", 0, s_qk),
                        (4 * cw + iw, 4 * cw + iw + LANES, "kiwi", 0, cfg.idx_heads ** -0.5)])
    sinks_c = tuple([((0, "T", j * half),) for j in range(2)]
                    + [((1, "row", j * half), (3, "row", j * half)) for j in range(2)]
                    + [((2, "T", 0), (3, "row", cw)), ((4, "row", 0),), ((5, "T", 0),),
                       ((6, "row_lo", 0), (7, "T_wi", 0), (8, "row", 0))])
    nc_pad = 4 * cw + iw + LANES

    def with_sinks(groups, sinks):
        return tuple(g + (s,) for g, s in zip(groups, sinks))

    def prompt_outs(width, extra):
        outs = [((nbp, width, lq), BF16, (None, width, tm), lambda i: (i // tpb, 0, i % tpb)),
                ((nbp, lq, width), BF16, (None, tm, width), lambda i: (i // tpb, i % tpb, 0)),
                ((nbp, gp.nk, width, tm), BF16, (None, None, width, tm), lambda i: (i // tpb, i % tpb, 0, 0)),
                ((nbp, lq, 2 * width), F32, (None, tm, 2 * width), lambda i: (i // tpb, i % tpb, 0)),
                ((nbp * lq, width), F32, (tm, width), lambda i: (i, 0))]
        return outs + extra

    extra_c = [((nbp, iw, lq), BF16, (None, iw, tm), lambda i: (i // tpb, 0, i % tpb)),
               ((nbp, lq, LANES), BF16, (None, tm, LANES), lambda i: (i // tpb, i % tpb, 0)),
               ((nbp, 8, lq), F32, (None, 8, tm), lambda i: (i // tpb, 0, i % tpb)),
               ((nbp, lq, LANES), F32, (None, tm, LANES), lambda i: (i // tpb, i % tpb, 0))]

    def sample_proj(w_bf, gn, gq, groups):
        outs = [((tm_s, w_bf.shape[1]), F32, (tm_s, w_bf.shape[1]), lambda i: (i, 0))]
        y = _proj(hs, gn, w_bf, cos_s, sin_s, gq, _row_sinks(groups), tm_s, outs)[0]
        return y, y[:rows_s].reshape(gs.nb, cfg.dec_seq, -1)

    def sample_rows(o, width):
        return jnp.pad(o[:, :cfg.dec_seq].reshape(rows_s, width), ((0, tm_s - rows_s), (0, 0)))

    rows_ab_p, rows_ab_s, rows_c_p, rows_c_s = [], [], [], []
    for l in range(cfg.depth):
        i = l // 2
        if l % 2 == 0:
            lam_init = 0.8 - 0.6 * float(np.exp(-0.3 * l))
            w = w_in_ab[i]
            wq = [w[:, j * aw:(j + 1) * aw] for j in range(4)] + [w[:, 4 * aw + j * bw:4 * aw + (j + 1) * bw] for j in range(4)]
            w_perm = jnp.concatenate([wq[0], wq[4], wq[1], wq[5], wq[2], wq[6], wq[3], wq[7]], axis=1).astype(BF16)
            gq = jnp.tile(g_qk_a[i], (1, 2))
            gq = jnp.pad(gq, ((0, 8 - gq.shape[0]), (0, 0)))
            w_out = w_out_ab[i].astype(BF16)
            gsub = g_sub_a[i][None, :]
            gn = g_norm_ab[i][None, :]
            a_shape, b_shape = (cfg.a_heads, 2 * HEAD_DIM), (cfg.b_heads, HEAD_DIM)

            qT, kbf, vT, kv, gates = _proj(hp, gn, w_perm, cos_p, sin_p, gq, with_sinks(groups_ab, sinks_ab), tm,
                                           prompt_outs(abw, []))
            new = kv[:, :t_real]
            rows_ab_p.append((new[..., 0:aw].reshape(nbp, t_real, *a_shape),
                              new[..., abw:abw + aw].reshape(nbp, t_real, *a_shape),
                              new[..., aw:abw].reshape(nbp, t_real, *b_shape),
                              new[..., abw + aw:2 * abw].reshape(nbp, t_real, *b_shape)))
            oa = _attn_a(gp, lam_a[i], qT, kbf, vT, cfg.a_heads, lam_init)
            ob = _attn_b(gp, qT, kbf, vT, cfg.b_heads // 2, cfg.a_heads)
            hp = _out_ab(oa.reshape(-1, aw), ob.reshape(-1, bw), gates, 0, hp, gsub, w_out, cfg.a_heads,
                         1.0 - lam_init, tm)

            y, new = sample_proj(w_perm, gn, gq, groups_ab)
            nbs = gs.nb
            rows_ab_s.append((new[..., abw:abw + aw].reshape(nbs, cfg.dec_seq, *a_shape),
                              new[..., 2 * abw:2 * abw + aw].reshape(nbs, cfg.dec_seq, *a_shape),
                              new[..., abw + aw:2 * abw].reshape(nbs, cfg.dec_seq, *b_shape),
                              new[..., 2 * abw + aw:3 * abw].reshape(nbs, cfg.dec_seq, *b_shape)))
            past_k = jnp.concatenate([cache_a_k[i].reshape(nbs, cfg.past_len, aw),
                                      cache_b_k[i].reshape(nbs, cfg.past_len, bw)], axis=-1)
            past_v = jnp.concatenate([cache_a_v[i].reshape(nbs, cfg.past_len, aw),
                                      cache_b_v[i].reshape(nbs, cfg.past_len, bw)], axis=-1)
            qT = _to_qT(new[..., 0:abw], gs)
            kbf = _to_k(jnp.concatenate([past_k, new[..., abw:2 * abw]], axis=1), gs)
            vT = _to_k(jnp.concatenate([past_v, new[..., 2 * abw:3 * abw]], axis=1), gs)
            oa = _attn_a(gs, lam_a[i], qT, kbf, vT, cfg.a_heads, lam_init)
            ob = _attn_b(gs, qT, kbf, vT, cfg.b_heads // 2, cfg.a_heads)
            hs = _out_ab(sample_rows(oa, aw), sample_rows(ob, bw), y, 3, hs, gsub, w_out, cfg.a_heads,
                         1.0 - lam_init, tm_s)
        else:
            w = w_in_c[i]
            w_pad = jnp.pad(w, ((0, 0), (0, nc_pad - w.shape[1]))).astype(BF16)
            gq = jnp.tile(g_qk_c[i], (1, 2))
            gq = jnp.pad(gq, ((0, 8 - gq.shape[0]), (0, 0)))
            w_out = w_out_c[i].astype(BF16)
            gn = g_norm_c[i][None, :]
            c_shape = (cfg.c_heads, HEAD_DIM)
            ki_col = 4 * cw + iw

            qT, kbf, vT, kv, gates, qiT, ki_bf, wiT, ki_f = _proj(hp, gn, w_pad, cos_p, sin_p, gq,
                                                                 with_sinks(groups_c, sinks_c), tm, prompt_outs(cw, extra_c))
            new = kv[:, :t_real]
            rows_c_p.append((new[..., 0:cw].reshape(nbp, t_real, *c_shape),
                             new[..., cw:2 * cw].reshape(nbp, t_real, *c_shape),
                             ki_f[:, :t_real, :HEAD_DIM]))
            oc = _dsa(gp, qiT, wiT, ki_bf, qT, kbf, vT, n_sel_p, cfg.idx_heads, cfg.c_heads // 2)
            hp = _out_c(oc.reshape(-1, cw), gates, 0, hp, w_out, tm)

            y, new = sample_proj(w_pad, gn, gq, groups_c)
            nbs = gs.nb
            ki_new = new[..., ki_col:ki_col + LANES]
            rows_c_s.append((new[..., cw:2 * cw].reshape(nbs, cfg.dec_seq, *c_shape),
                             new[..., 2 * cw:3 * cw].reshape(nbs, cfg.dec_seq, *c_shape),
                             ki_new[..., :HEAD_DIM]))
            k_all = jnp.concatenate([cache_c_k[i].reshape(nbs, cfg.past_len, cw), new[..., cw:2 * cw]], axis=1)
            v_all = jnp.concatenate([cache_c_v[i].reshape(nbs, cfg.past_len, cw), new[..., 2 * cw:3 * cw]], axis=1)
            ki_past = jnp.pad(cache_c_kidx[i], ((0, 0), (0, 0), (0, LANES - HEAD_DIM)))
            ki_all = jnp.concatenate([ki_past, jnp.where(jnp.arange(LANES) < HEAD_DIM, ki_new, 0.0)], axis=1)
            wi_new = ki_new[..., HEAD_DIM:HEAD_DIM + 8]
            wiT = jnp.pad(jnp.swapaxes(wi_new, 1, 2), ((0, 0), (0, 0), (0, gs.lq - cfg.dec_seq)))
            oc = _dsa(gs, _to_qT(new[..., 4 * cw:ki_col], gs), wiT, _to_k(ki_all, gs), _to_qT(new[..., 0:cw], gs),
                      _to_k(k_all, gs), _to_k(v_all, gs), n_sel_s, cfg.idx_heads, cfg.c_heads // 2)
            hs = _out_c(sample_rows(oc, cw), y, 3, hs, w_out, tm_s)

    y_prompt = hp.reshape(cfg.batch, lq, d)[:, cfg.n_meta:t_real]
    y_sample = hs[:rows_s].reshape(cfg.dec_batch, cfg.dec_seq, d)

    def stack(rows, j):
        return jnp.stack([r[j] for r in rows], axis=0)

    return (y_prompt, y_sample,
            stack(rows_ab_p, 0), stack(rows_ab_p, 1), stack(rows_ab_p, 2), stack(rows_ab_p, 3),
            stack(rows_c_p, 0), stack(rows_c_p, 1), stack(rows_c_p, 2),
            stack(rows_ab_s, 0), stack(rows_ab_s, 1), stack(rows_ab_s, 2), stack(rows_ab_s, 3),
            stack(rows_c_s, 0), stack(rows_c_s, 1), stack(rows_c_s, 2))


def kernel(x_prompt, x_sample, cache_a_k, cache_a_v, cache_b_k, cache_b_v, cache_c_k, cache_c_v, cache_c_kidx,
           meta_tokens, g_norm_ab, w_in_ab, g_qk_a, lam_a, g_sub_a, w_out_ab, g_norm_c, w_in_c, g_qk_c, w_out_c):
    return _forward(Cfg(), x_prompt, x_sample, cache_a_k, cache_a_v, cache_b_k, cache_b_v, cache_c_k, cache_c_v,
                    cache_c_kidx, meta_tokens, g_norm_ab, w_in_ab, g_qk_a, lam_a, g_sub_a, w_out_ab,
                    g_norm_c, w_in_c, g_qk_c, w_out_c)
```
